```python
import jax, jax.numpy as jnp
from jax import lax
import numpy as np


D_MODEL = 1024
BATCH = 16
SEQ = 2048
DEPTH = 1

HEAD_DIM = 64
RWKV_HEADS = 8
RWKV_WIDTH = RWKV_HEADS * HEAD_DIM
DECAY_LORA = 32
AAA_LORA = 32
ATTN_HEADS = 8
ATTN_WIDTH = ATTN_HEADS * HEAD_DIM
MIX_WIDTH = RWKV_WIDTH + ATTN_WIDTH
Q_RANK = 256
KV_RANK = 128
IDX_HEADS = 4
IDX_DIM = 64
TOPK_MAX = 256
Q_BLOCK = 128
NORM_EPS = 1e-6
LN_EPS = 1e-5
GN_EPS = 64e-5

SHIFT_SIZES = [RWKV_WIDTH, RWKV_WIDTH, RWKV_WIDTH, DECAY_LORA, AAA_LORA]
SHIFT_COLS = sum(SHIFT_SIZES)
IN_SIZES = [SHIFT_COLS, RWKV_WIDTH, Q_RANK, KV_RANK, IDX_DIM, IDX_HEADS, ATTN_WIDTH]
IN_COLS = sum(IN_SIZES)
SHIFT_SPLITS = np.cumsum(SHIFT_SIZES)[:-1].tolist()
IN_SPLITS = np.cumsum(IN_SIZES)[:-1].tolist()

kernel_name = "hymba_rwkv7_dsa_hybrid"


def rmsnorm(x, g):
    xf = x.astype(jnp.float32)
    y = xf * lax.rsqrt(jnp.mean(xf * xf, axis=-1, keepdims=True) + NORM_EPS)
    return (y * g.astype(jnp.float32)).astype(x.dtype)


def layernorm(x, g, b):
    xf = x.astype(jnp.float32)
    mu = jnp.mean(xf, axis=-1, keepdims=True)
    var = jnp.mean(jnp.square(xf - mu), axis=-1, keepdims=True)
    y = (xf - mu) * lax.rsqrt(var + LN_EPS)
    return (y * g.astype(jnp.float32) + b.astype(jnp.float32)).astype(x.dtype)


def rwkv7_scan(r, decay, k, v, kk, a):
    B, T, H, D = r.shape
    tm = lambda t: jnp.moveaxis(t, 1, 0)

    def step(S, inp):
        r_t, w_t, k_t, v_t, kk_t, a_t = inp
        sa = jnp.einsum('bhij,bhj->bhi', S, -kk_t)
        S = (S * w_t[:, :, None, :]
             + sa[..., None] * (kk_t * a_t)[:, :, None, :]
             + v_t[..., None] * k_t[:, :, None, :])
        return S, jnp.einsum('bhij,bhj->bhi', S, r_t)

    S0 = jnp.zeros((B, H, D, D), jnp.float32)
    _, y = lax.scan(step, S0, (tm(r), tm(decay), tm(k), tm(v), tm(kk), tm(a)))
    return jnp.moveaxis(y, 0, 1)


def rwkv7_branch(p_shift, gate, mu_shift, w0, w_up, a0, a_up, k_k, k_a, r_k, gn_g, gn_b):
    B, T, _ = p_shift.shape
    f32 = jnp.float32
    prev = jnp.pad(p_shift, ((0, 0), (1, 0), (0, 0)))[:, :-1]
    xs = p_shift + (prev - p_shift) * mu_shift
    r, k, v, wd, ad = jnp.split(xs, SHIFT_SPLITS, axis=-1)
    wlog = -jax.nn.softplus(-(w0 + jnp.tanh(wd) @ w_up)) - 0.5
    decay = jnp.exp(-jnp.exp(wlog.astype(f32)))
    a = jax.nn.sigmoid((a0 + ad @ a_up).astype(f32))
    heads = lambda t: t.astype(f32).reshape(B, T, RWKV_HEADS, HEAD_DIM)
    r, k, v, a, decay = heads(r), heads(k), heads(v), heads(a), heads(decay)
    kk = k * k_k.astype(f32)
    kk = kk / jnp.maximum(jnp.sqrt(jnp.sum(kk * kk, axis=-1, keepdims=True)), 1e-12)
    k = k * (1.0 + (a - 1.0) * k_a.astype(f32))
    y = rwkv7_scan(r, decay, k, v, kk, a)
    mu = jnp.mean(y, axis=-1, keepdims=True)
    var = jnp.mean(jnp.square(y - mu), axis=-1, keepdims=True)
    y = (y - mu) * lax.rsqrt(var + GN_EPS) * gn_g.astype(f32) + gn_b.astype(f32)
    y = y + jnp.sum(r * k * r_k.astype(f32), axis=-1, keepdims=True) * v
    y = y.reshape(B, T, RWKV_WIDTH).astype(gate.dtype)
    return y * jax.nn.silu(gate)


def dsa_branch(q_down, kv_down, k_idx_raw, w_idx, gate, q_norm_g, kv_norm_g,
               w_uq, w_uk, w_uv, w_qidx, kidx_g, kidx_b):
    B, T, _ = q_down.shape
    f32 = jnp.float32
    c_q = rmsnorm(q_down, q_norm_g)
    c_kv = rmsnorm(kv_down, kv_norm_g)
    q = (c_q @ w_uq).reshape(B, T, ATTN_HEADS, HEAD_DIM)
    q_abs = jnp.einsum('bthd,hrd->bthr', q, w_uk) * (HEAD_DIM ** -0.5)
    q_idx = (c_q @ w_qidx).reshape(B, T, IDX_HEADS, IDX_DIM)
    k_idx = layernorm(k_idx_raw, kidx_g, kidx_b).astype(f32)
    w_i = w_idx * (IDX_HEADS ** -0.5)
    topk = min(TOPK_MAX, T // 4)
    nb = T // Q_BLOCK
    blocks = lambda t: jnp.moveaxis(t.reshape(B, nb, Q_BLOCK, *t.shape[2:]), 1, 0)
    key_pos = jnp.arange(T)

    def attend(args):
        qa, qi, wi, t0 = args
        q_pos = t0 + jnp.arange(Q_BLOCK)
        causal = key_pos[None, :] <= q_pos[:, None]
        logits = jnp.einsum('bqhd,bsd->bqhs', qi.astype(f32), k_idx)
        score = jnp.einsum('bqh,bqhs->bqs', wi.astype(f32), jax.nn.relu(logits))
        score = jnp.where(causal[None], score, -jnp.inf)
        _, idx = lax.top_k(score, topk)
        c_sel = jax.vmap(lambda c, i: c[i])(c_kv, idx).astype(f32)
        valid = idx <= q_pos[None, :, None]
        s = jnp.einsum('bqhr,bqkr->bhqk', qa.astype(f32), c_sel)
        s = jnp.where(valid[:, None], s, -jnp.inf)
        p = jax.nn.softmax(s, axis=-1)
        return jnp.einsum('bhqk,bqkr->bqhr', p, c_sel)

    o_lat = lax.map(attend, (blocks(q_abs), blocks(q_idx), blocks(w_i), jnp.arange(nb) * Q_BLOCK))
    o_lat = jnp.moveaxis(o_lat, 0, 1).reshape(B, T, ATTN_HEADS, KV_RANK).astype(gate.dtype)
    o = jnp.einsum('bthr,hrd->bthd', o_lat, w_uv).reshape(B, T, ATTN_WIDTH)
    return o * jax.nn.silu(gate)


def hybrid_layer(x, norm_g, w_in, mu_shift, w0, w_up, a0, a_up, k_k, k_a, r_k, gn_g, gn_b,
                 q_norm_g, kv_norm_g, w_uq, w_uk, w_uv, w_qidx, kidx_g, kidx_b, w_out):
    xn = rmsnorm(x, norm_g)
    p = xn @ w_in
    p_shift, g_rwkv, q_down, kv_down, k_idx_raw, w_idx, g_attn = jnp.split(p, IN_SPLITS, axis=-1)
    y_rwkv = rwkv7_branch(p_shift, g_rwkv, mu_shift, w0, w_up, a0, a_up, k_k, k_a, r_k, gn_g, gn_b)
    y_attn = dsa_branch(q_down, kv_down, k_idx_raw, w_idx, g_attn, q_norm_g, kv_norm_g,
                        w_uq, w_uk, w_uv, w_qidx, kidx_g, kidx_b)
    y = jnp.concatenate([y_rwkv, y_attn], axis=-1) @ w_out
    return x + y


def setup_inputs(seed: int = 0) -> dict:
    key = jax.random.key(seed)
    ks = jax.random.split(key, 24)
    L = DEPTH
    f32 = jnp.float32
    nrm = lambda k, shape, scale: scale * jax.random.normal(k, shape, f32)
    gain = lambda k, shape: 1.0 + 0.05 * jax.random.normal(k, shape, f32)
    hd = (L, RWKV_HEADS, HEAD_DIM)
    return {
        "x": nrm(ks[0], (BATCH, SEQ, D_MODEL), 1.0),
        "norm_g": gain(ks[1], (L, D_MODEL)),
        "w_in": nrm(ks[2], (L, D_MODEL, IN_COLS), D_MODEL ** -0.5),
        "mu_shift": jax.random.uniform(ks[3], (L, SHIFT_COLS), f32),
        "w0": jax.random.uniform(ks[4], (L, RWKV_WIDTH), f32, -2.0, 2.0),
        "w_up": nrm(ks[5], (L, DECAY_LORA, RWKV_WIDTH), 0.5 * DECAY_LORA ** -0.5),
        "a0": nrm(ks[6], (L, RWKV_WIDTH), 0.5),
        "a_up": nrm(ks[7], (L, AAA_LORA, RWKV_WIDTH), 0.5 * AAA_LORA ** -0.5),
        "k_k": 0.85 + 0.05 * jax.random.normal(ks[8], hd, f32),
        "k_a": gain(ks[9], hd),
        "r_k": nrm(ks[10], hd, 0.1),
        "gn_g": gain(ks[11], hd),
        "gn_b": nrm(ks[12], hd, 0.02),
        "q_norm_g": gain(ks[13], (L, Q_RANK)),
        "kv_norm_g": gain(ks[14], (L, KV_RANK)),
        "w_uq": nrm(ks[15], (L, Q_RANK, ATTN_WIDTH), Q_RANK ** -0.5),
        "w_uk": nrm(ks[16], (L, ATTN_HEADS, KV_RANK, HEAD_DIM), KV_RANK ** -0.5),
        "w_uv": nrm(ks[17], (L, ATTN_HEADS, KV_RANK, HEAD_DIM), KV_RANK ** -0.5),
        "w_qidx": nrm(ks[18], (L, Q_RANK, IDX_HEADS * IDX_DIM), Q_RANK ** -0.5),
        "kidx_g": gain(ks[19], (L, IDX_DIM)),
        "kidx_b": nrm(ks[20], (L, IDX_DIM), 0.02),
        "w_out": nrm(ks[21], (L, MIX_WIDTH, D_MODEL), MIX_WIDTH ** -0.5),
        "final_g": gain(ks[22], (D_MODEL,)),
    }


def reference(x, norm_g, w_in, mu_shift, w0, w_up, a0, a_up, k_k, k_a, r_k, gn_g, gn_b,
              q_norm_g, kv_norm_g, w_uq, w_uk, w_uv, w_qidx, kidx_g, kidx_b, w_out, final_g):
    for l in range(DEPTH):
        x = hybrid_layer(x, norm_g[l], w_in[l], mu_shift[l], w0[l], w_up[l], a0[l], a_up[l],
                         k_k[l], k_a[l], r_k[l], gn_g[l], gn_b[l], q_norm_g[l], kv_norm_g[l],
                         w_uq[l], w_uk[l], w_uv[l], w_qidx[l], kidx_g[l], kidx_b[l], w_out[l])
    return rmsnorm(x, final_g)
```

```python
import functools

import numpy as np
import jax
import jax.numpy as jnp
from jax import lax
from jax.experimental import pallas as pl
from jax.experimental.pallas import tpu as pltpu

F32 = jnp.float32
BF16 = jnp.bfloat16
I32 = jnp.int32

HEAD_DIM = 64
RWKV_HEADS = 8
RWKV_WIDTH = RWKV_HEADS * HEAD_DIM
DECAY_LORA = 32
AAA_LORA = 32
ATTN_HEADS = 8
ATTN_WIDTH = ATTN_HEADS * HEAD_DIM
Q_RANK = 256
KV_RANK = 128
IDX_HEADS = 4
IDX_DIM = 64
TOPK_MAX = 256
NORM_EPS = 1e-6
LN_EPS = 1e-5
GN_EPS = 64e-5

LANES = 128
SUBLANES = 8
QB = 128
CHUNK = 64
PAIR = 2 * HEAD_DIM

C_RKV = 0
C_GR = 3 * RWKV_WIDTH
C_GA = C_GR + RWKV_WIDTH
C_QD = C_GA + ATTN_WIDTH
C_KV = C_QD + Q_RANK
C_SM = C_KV + KV_RANK
C_WI = C_SM + LANES
P_COLS = C_WI + LANES

VMEM_LIMIT = 56 * 1024 * 1024


def _params(sem):
    return pltpu.CompilerParams(dimension_semantics=sem, vmem_limit_bytes=VMEM_LIMIT)


def _dot(a, b):
    return jnp.dot(a, b, preferred_element_type=F32)


def _dot_nt(a, b):
    return lax.dot_general(a, b, (((1,), (1,)), ((), ())), preferred_element_type=F32)


def _dot_tn(a, b):
    return lax.dot_general(a, b, (((0,), (0,)), ((), ())), preferred_element_type=F32)


def _split2(x):
    hi = x.astype(BF16)
    lo = (x - hi.astype(F32)).astype(BF16)
    return hi, lo


def _split3(x):
    hi = x.astype(BF16)
    r1 = x - hi.astype(F32)
    mid = r1.astype(BF16)
    lo = (r1 - mid.astype(F32)).astype(BF16)
    return hi, mid, lo


def _dot_exact_rhs(x, m_bf16, parts=2):
    if parts == 2:
        hi, lo = _split2(x)
        return _dot(hi, m_bf16) + _dot(lo, m_bf16)
    hi, mid, lo = _split3(x)
    return _dot(hi, m_bf16) + _dot(mid, m_bf16) + _dot(lo, m_bf16)


def _fold8(x, op):
    acc = x[0:SUBLANES]
    for j in range(1, x.shape[0] // SUBLANES):
        acc = op(acc, x[j * SUBLANES:(j + 1) * SUBLANES])
    return acc


def _colsum(x):
    return jnp.sum(_fold8(x, jnp.add), axis=0, keepdims=True)


def _colmax(x):
    return jnp.max(_fold8(x, jnp.maximum), axis=0, keepdims=True)


def _sigmoid(x):
    return 1.0 / (1.0 + jnp.exp(-x))


def _absorb_kernel(wuk_ref, wuq_ref, o_ref):
    a = wuk_ref[0]
    b = wuq_ref[0]
    ah, al = _split2(a)
    bh, bl = _split2(b)
    acc = _dot_nt(ah, bh) + _dot_nt(ah, bl) + _dot_nt(al, bh)
    o_ref[0] = acc * (HEAD_DIM ** -0.5)


def _absorb(w_uk, w_uq_heads):
    return pl.pallas_call(
        _absorb_kernel,
        grid=(ATTN_HEADS,),
        in_specs=[pl.BlockSpec((1, KV_RANK, HEAD_DIM), lambda h: (h, 0, 0)),
                  pl.BlockSpec((1, Q_RANK, HEAD_DIM), lambda h: (h, 0, 0))],
        out_specs=pl.BlockSpec((1, KV_RANK, Q_RANK), lambda h: (h, 0, 0)),
        out_shape=jax.ShapeDtypeStruct((ATTN_HEADS, KV_RANK, Q_RANK), F32),
        compiler_params=_params(("arbitrary",)),
        name="absorb",
    )(w_uk, w_uq_heads)


def _inproj_kernel(x_ref, g_ref, w_ref, o_ref):
    x = x_ref[...]
    ms = jnp.mean(x * x, axis=-1, keepdims=True)
    xn = x * lax.rsqrt(ms + NORM_EPS) * g_ref[...]
    o_ref[...] = _dot(xn.astype(BF16), w_ref[...])


def _inproj(x2, norm_g, w_all, tm):
    n, d = x2.shape
    return pl.pallas_call(
        _inproj_kernel,
        grid=(n // tm,),
        in_specs=[pl.BlockSpec((tm, d), lambda i: (i, 0)),
                  pl.BlockSpec((1, d), lambda i: (0, 0)),
                  pl.BlockSpec((d, P_COLS), lambda i: (0, 0))],
        out_specs=pl.BlockSpec((tm, P_COLS), lambda i: (i, 0)),
        out_shape=jax.ShapeDtypeStruct((n, P_COLS), F32),
        compiler_params=_params(("parallel",)),
        name="inproj",
    )(x2, norm_g, w_all)


def _rwkv_kernel(rkv_ref, sm_ref, gate_ref, mu_rkv_ref, mu_sm_ref, w0_ref, wup_ref, a0_ref,
                 aup_ref, kkw_ref, ka_ref, rk_ref, gng_ref, gnb_ref, bd_ref, tri_ref,
                 o_ref,
                 carry_rkv, carry_sm, s_scr, n_scr, r_scr, u_scr, k_scr, v_scr, wt_scr, y_scr,
                 *, tb):
    t = pl.program_id(1)
    W = RWKV_WIDTH

    @pl.when(t == 0)
    def _():
        carry_rkv[...] = jnp.zeros_like(carry_rkv)
        carry_sm[...] = jnp.zeros_like(carry_sm)
        s_scr[...] = jnp.zeros_like(s_scr)

    p = rkv_ref[...]
    row = lax.broadcasted_iota(I32, (tb, 1), 0)
    prev = jnp.where(row == 0, carry_rkv[0:1, :], pltpu.roll(p, 1, 0))
    carry_rkv[0:1, :] = p[tb - 1:tb, :]
    xs = p + (prev - p) * mu_rkv_ref[...]
    r = xs[:, 0:W]
    k = xs[:, W:2 * W]
    v = xs[:, 2 * W:3 * W]

    ps = sm_ref[...]
    prev_s = jnp.where(row == 0, carry_sm[0:1, :], pltpu.roll(ps, 1, 0))
    carry_sm[0:1, :] = ps[tb - 1:tb, :]
    xl = ps + (prev_s - ps) * mu_sm_ref[...]

    zw = w0_ref[...] + _dot(jnp.tanh(xl).astype(BF16), wup_ref[...])
    logw = -np.float32(np.exp(-0.5)) * _sigmoid(zw)
    a = _sigmoid(a0_ref[...] + _dot(xl.astype(BF16), aup_ref[...]))

    bd = bd_ref[...]
    kk = k * kkw_ref[...]
    ss = _dot_exact_rhs(kk * kk, bd)
    kk = kk * (1.0 / jnp.maximum(jnp.sqrt(ss), 1e-12))
    k2 = k * (1.0 + (a - 1.0) * ka_ref[...])
    bonus = _dot_exact_rhs(r * k2 * rk_ref[...], bd)

    hi, mid, lo = _split3(logw)
    tri = tri_ref[...]
    cum = _dot(tri, hi) + _dot(tri, mid) + _dot(tri, lo)
    wt = jnp.exp(cum)
    wi = jnp.exp(-cum)
    wx = jnp.exp(cum - logw)
    n_scr[...] = (-kk * wx).astype(BF16)
    r_scr[...] = (r * wt).astype(BF16)
    u_scr[...] = (kk * a * wi).astype(BF16)
    k_scr[...] = (k2 * wi).astype(BF16)
    v_scr[...] = v.astype(BF16)
    wt_scr[...] = wt

    rowp = lax.broadcasted_iota(I32, (PAIR, PAIR), 0)
    colp = lax.broadcasted_iota(I32, (PAIR, PAIR), 1)
    same_blk = (rowp >= HEAD_DIM) == (colp >= HEAD_DIM)
    strict_lo = (colp % HEAD_DIM) < (rowp % HEAD_DIM)
    m_abd = same_blk & strict_lo
    m_ank = jnp.logical_not(same_blk) & strict_lo
    rowc = lax.broadcasted_iota(I32, (CHUNK, PAIR), 0)
    colc = lax.broadcasted_iota(I32, (CHUNK, PAIR), 1)
    incl_lo = (colc % HEAD_DIM) <= rowc
    head0_c = colc < HEAD_DIM
    head0_p = colp < HEAD_DIM

    def chunk_body(c, carry):
        off = pl.multiple_of(c * CHUNK, CHUNK)
        last8 = pl.multiple_of(c * CHUNK + CHUNK - SUBLANES, SUBLANES)
        for g in range(RWKV_HEADS // 2):
            sl = slice(g * PAIR, (g + 1) * PAIR)
            nn = n_scr[pl.ds(off, CHUNK), sl]
            rr = r_scr[pl.ds(off, CHUNK), sl]
            uu = u_scr[pl.ds(off, CHUNK), sl]
            kc = k_scr[pl.ds(off, CHUNK), sl]
            vv = v_scr[pl.ds(off, CHUNK), sl]
            wc = wt_scr[pl.ds(last8, SUBLANES), sl][SUBLANES - 1:SUBLANES, :]

            nr = jnp.concatenate([nn, rr], axis=0)
            zero = jnp.zeros_like(nr)
            nr0 = jnp.where(head0_p, nr, zero)
            nr1 = jnp.where(head0_p, zero, nr)
            uk = jnp.concatenate([uu, kc], axis=0)
            ku = jnp.concatenate([kc, uu], axis=0)
            g0 = _dot_nt(nr0, uk)
            g1 = _dot_nt(nr1, ku)
            s_old = s_scr[g]
            nrh = _dot_nt(nr, s_old.astype(BF16))
            nh = nrh[0:CHUNK]
            rh = nrh[CHUNK:PAIR]

            gtop = jnp.concatenate([g0[0:CHUNK], g1[0:CHUNK]], axis=0)
            abd = jnp.where(m_abd, gtop, 0.0)
            ank = jnp.where(m_ank, gtop, 0.0)
            vvv = jnp.concatenate([vv, vv], axis=0)
            zf = jnp.concatenate([nh, nh], axis=0) + _dot(ank.astype(BF16), vvv)
            pw = abd
            for it in range(6):
                pwb = pw.astype(BF16)
                zf = zf + _dot(pwb, zf.astype(BF16))
                if it < 5:
                    pw = _dot(pwb, pwb)
            z = jnp.where(head0_c, zf[0:CHUNK], zf[CHUNK:PAIR])
            zb = z.astype(BF16)
            zv = jnp.concatenate([zb, vv], axis=0)
            vz = jnp.concatenate([vv, zb], axis=0)
            ar0 = jnp.where(incl_lo, g0[CHUNK:PAIR], 0.0).astype(BF16)
            ar1 = jnp.where(incl_lo, g1[CHUNK:PAIR], 0.0).astype(BF16)
            y = rh + jnp.where(head0_c, _dot(ar0, zv), _dot(ar1, vz))
            y_scr[pl.ds(off, CHUNK), sl] = y
            s_new = (s_old + _dot_tn(zv, uk)) * wc
            s_scr[g] = jnp.where(same_blk, s_new, 0.0)
        return carry

    lax.fori_loop(0, tb // CHUNK, chunk_body, 0)

    y = y_scr[...]
    inv_d = np.float32(1.0 / HEAD_DIM)
    mu = _dot_exact_rhs(y, bd) * inv_d
    yc = y - mu
    var = _dot_exact_rhs(yc * yc, bd) * inv_d
    y = yc * lax.rsqrt(var + GN_EPS) * gng_ref[...] + gnb_ref[...]
    y = y + bonus * v
    gt = gate_ref[...]
    o_ref[...] = y * (gt * _sigmoid(gt))


def _rwkv(p, b, t, tb, consts):
    nt = t // tb
    W = RWKV_WIDTH
    row_spec = lambda width, cb: pl.BlockSpec((tb, width), lambda bi, ti: (bi * nt + ti, cb))
    full = lambda arr: pl.BlockSpec(arr.shape, lambda bi, ti: (0,) * arr.ndim)
    in_specs = [row_spec(3 * W, C_RKV // (3 * W)), row_spec(LANES, C_SM // LANES),
                row_spec(W, C_GR // W)] + [full(c) for c in consts]
    return pl.pallas_call(
        functools.partial(_rwkv_kernel, tb=tb),
        grid=(b, nt),
        in_specs=in_specs,
        out_specs=pl.BlockSpec((tb, W), lambda bi, ti: (bi * nt + ti, 0)),
        out_shape=jax.ShapeDtypeStruct((b * t, W), F32),
        scratch_shapes=[pltpu.VMEM((SUBLANES, 3 * W), F32), pltpu.VMEM((SUBLANES, LANES), F32),
                        pltpu.VMEM((RWKV_HEADS // 2, PAIR, PAIR), F32),
                        pltpu.VMEM((tb, W), BF16), pltpu.VMEM((tb, W), BF16),
                        pltpu.VMEM((tb, W), BF16), pltpu.VMEM((tb, W), BF16),
                        pltpu.VMEM((tb, W), BF16), pltpu.VMEM((tb, W), F32),
                        pltpu.VMEM((tb, W), F32)],
        compiler_params=_params(("parallel", "arbitrary")),
        name="rwkv",
    )(p, p, p, *consts)


def _dsaprep_kernel(qd_ref, kv_ref, sm_ref, wi_ref, qg_ref, kvg_ref, lng_ref, lnb_ref,
                    wqidx_ref, wabs_ref, wuv_ref,
                    ckv_ref, kidx_ref, vt_ref, qidx_ref, qabs_ref, wit_ref, *, tb):
    nq = tb // QB
    qd = qd_ref[...]
    cq = qd * lax.rsqrt(jnp.mean(qd * qd, axis=-1, keepdims=True) + NORM_EPS) * qg_ref[...]
    cqb = cq.astype(BF16)
    kv = kv_ref[...]
    ckv = kv * lax.rsqrt(jnp.mean(kv * kv, axis=-1, keepdims=True) + NORM_EPS) * kvg_ref[...]
    ckvb = ckv.astype(BF16)

    sm = sm_ref[...]
    lane = lax.broadcasted_iota(I32, sm.shape, 1)
    hi_half = lane >= IDX_DIM
    inv_d = np.float32(1.0 / IDX_DIM)
    mu = jnp.sum(jnp.where(hi_half, sm, 0.0), axis=-1, keepdims=True) * inv_d
    xc = jnp.where(hi_half, sm - mu, 0.0)
    var = jnp.sum(xc * xc, axis=-1, keepdims=True) * inv_d
    kidx = (xc * lax.rsqrt(var + LN_EPS) * lng_ref[...] + lnb_ref[...]).astype(BF16)

    vt = _dot_nt(wuv_ref[...], ckvb).astype(BF16)
    wit = jnp.transpose(wi_ref[...])[0:SUBLANES, :] * np.float32(IDX_HEADS ** -0.5)
    for j in range(nq):
        qs = slice(j * QB, (j + 1) * QB)
        ckv_ref[0, j] = ckvb[qs, :]
        kidx_ref[0, j] = kidx[qs, :]
        vt_ref[0, j] = vt[:, qs]
    wit_ref[0] = wit
    for h in range(IDX_HEADS):
        res = _dot_nt(wqidx_ref[h], cqb).astype(BF16)
        for j in range(nq):
            qidx_ref[0, :, (j * IDX_HEADS + h) * QB:(j * IDX_HEADS + h + 1) * QB] = \
                res[:, j * QB:(j + 1) * QB]
    for h in range(ATTN_HEADS):
        res = _dot_nt(wabs_ref[h].astype(BF16), cqb).astype(BF16)
        for j in range(nq):
            qabs_ref[0, :, (j * ATTN_HEADS + h) * QB:(j * ATTN_HEADS + h + 1) * QB] = \
                res[:, j * QB:(j + 1) * QB]


def _dsaprep(p, b, t, tb, consts):
    nt = t // tb
    nb = t // QB
    nq = tb // QB
    row_spec = lambda width, cb: pl.BlockSpec((tb, width), lambda bi, ti: (bi * nt + ti, cb))
    full = lambda arr: pl.BlockSpec(arr.shape, lambda bi, ti: (0,) * arr.ndim)
    in_specs = [row_spec(Q_RANK, C_QD // Q_RANK), row_spec(KV_RANK, C_KV // KV_RANK),
                row_spec(LANES, C_SM // LANES), row_spec(LANES, C_WI // LANES)] + \
               [full(c) for c in consts]
    out_shape = [jax.ShapeDtypeStruct((b, nb, QB, KV_RANK), BF16),
                 jax.ShapeDtypeStruct((b, nb, QB, LANES), BF16),
                 jax.ShapeDtypeStruct((b, nb, ATTN_WIDTH, QB), BF16),
                 jax.ShapeDtypeStruct((b, LANES, nb * IDX_HEADS * QB), BF16),
                 jax.ShapeDtypeStruct((b, KV_RANK, nb * ATTN_HEADS * QB), BF16),
                 jax.ShapeDtypeStruct((b, SUBLANES, t), F32)]
    out_specs = [pl.BlockSpec((1, nq, QB, KV_RANK), lambda bi, ti: (bi, ti, 0, 0)),
                 pl.BlockSpec((1, nq, QB, LANES), lambda bi, ti: (bi, ti, 0, 0)),
                 pl.BlockSpec((1, nq, ATTN_WIDTH, QB), lambda bi, ti: (bi, ti, 0, 0)),
                 pl.BlockSpec((1, LANES, nq * IDX_HEADS * QB), lambda bi, ti: (bi, 0, ti)),
                 pl.BlockSpec((1, KV_RANK, nq * ATTN_HEADS * QB), lambda bi, ti: (bi, 0, ti)),
                 pl.BlockSpec((1, SUBLANES, tb), lambda bi, ti: (bi, 0, ti))]
    return pl.pallas_call(
        functools.partial(_dsaprep_kernel, tb=tb),
        grid=(b, nt),
        in_specs=in_specs,
        out_specs=out_specs,
        out_shape=out_shape,
        compiler_params=_params(("parallel", "parallel")),
        name="dsaprep",
    )(p, p, p, p, *consts)


INT_MIN = np.int32(-2 ** 31)


def _dsa_kernel(kidx_ref, ckv_ref, vt_ref, qidx_ref, qabs_ref, wi_ref, gate_ref, tri_ref,
                o_ref, key_scr, s_scr, acc_scr, *, topk):
    i = pl.program_id(1)
    nch = i + 1
    row = lax.broadcasted_iota(I32, (QB, QB), 0)
    col = lax.broadcasted_iota(I32, (QB, QB), 1)
    qpos = i * QB + col
    wi = wi_ref[0]

    def score_body(c, carry):
        kc = kidx_ref[0, c]
        sc = jnp.zeros((QB, QB), F32)
        for h in range(IDX_HEADS):
            lg = _dot(kc, qidx_ref[0, :, h * QB:(h + 1) * QB])
            sc = sc + wi[h:h + 1, :] * jnp.maximum(lg, 0.0)
        sc = jnp.where(c * QB + row <= qpos, sc, -jnp.inf)
        bits = lax.bitcast_convert_type(sc, I32)
        key = jnp.where(bits < 0, bits ^ np.int32(0x7FFFFFFF), bits)
        key_scr[c] = jnp.where(sc == 0.0, 0, key)
        return carry

    lax.fori_loop(0, nch, score_body, 0)

    def count(pred):
        def body(c, acc):
            return acc + _fold8(pred(key_scr[c]).astype(I32), jnp.add)
        acc = lax.fori_loop(0, nch, body, jnp.zeros((SUBLANES, QB), I32))
        return jnp.sum(acc, axis=0, keepdims=True)

    thr = jnp.full((1, QB), INT_MIN, I32)
    cand = jnp.zeros((1, QB), I32)
    thr = jnp.where(count(lambda k: k >= cand) >= topk, cand, thr)
    for bit in range(30, -1, -1):
        cand = thr | np.int32(1 << bit)
        thr = jnp.where(count(lambda k, cand=cand: k >= cand) >= topk, cand, thr)
    need = (topk - count(lambda k: k > thr)).astype(F32)

    tri = tri_ref[...]

    def pass1(c, carry):
        run, ms = carry
        key = key_scr[c]
        eq = key == thr
        eqf = eq.astype(F32)
        before = _dot(tri, eqf.astype(BF16)) + run
        sel = (key > thr) | (eq & (before < need))
        sel = sel & (c * QB + row <= qpos)
        run = run + _colsum(eqf)
        s = _dot(ckv_ref[0, c], qabs_ref[0])
        new_ms = []
        for h in range(ATTN_HEADS):
            sh = jnp.where(sel, s[:, h * QB:(h + 1) * QB], -jnp.inf)
            s_scr[c, :, h * QB:(h + 1) * QB] = sh
            new_ms.append(jnp.maximum(ms[h], _colmax(sh)))
        return run, tuple(new_ms)

    ninf = jnp.full((1, QB), -jnp.inf, F32)
    _, ms = lax.fori_loop(0, nch, pass1,
                          (jnp.zeros((1, QB), F32), tuple(ninf for _ in range(ATTN_HEADS))))

    acc_scr[...] = jnp.zeros_like(acc_scr)

    def pass2(c, ls):
        new_ls = []
        for h in range(ATTN_HEADS):
            p = jnp.exp(s_scr[c, :, h * QB:(h + 1) * QB] - ms[h])
            new_ls.append(ls[h] + _colsum(p))
            hs = slice(h * HEAD_DIM, (h + 1) * HEAD_DIM)
            acc_scr[hs, :] += _dot(vt_ref[0, c, hs, :], p.astype(BF16))
        return tuple(new_ls)

    zero = jnp.zeros((1, QB), F32)
    ls = lax.fori_loop(0, nch, pass2, tuple(zero for _ in range(ATTN_HEADS)))

    for h in range(ATTN_HEADS):
        hs = slice(h * HEAD_DIM, (h + 1) * HEAD_DIM)
        acc_scr[hs, :] = acc_scr[hs, :] * (1.0 / ls[h])
    o = jnp.transpose(acc_scr[...])
    gt = gate_ref[...]
    o_ref[...] = o * (gt * _sigmoid(gt))


def _dsa(p, prep, tri, b, t, topk):
    nb = t // QB
    ckv, kidx, vt, qidx, qabs, wit = prep
    in_specs = [pl.BlockSpec((1, nb, QB, LANES), lambda bi, qi: (bi, 0, 0, 0)),
                pl.BlockSpec((1, nb, QB, KV_RANK), lambda bi, qi: (bi, 0, 0, 0)),
                pl.BlockSpec((1, nb, ATTN_WIDTH, QB), lambda bi, qi: (bi, 0, 0, 0)),
                pl.BlockSpec((1, LANES, IDX_HEADS * QB), lambda bi, qi: (bi, 0, qi)),
                pl.BlockSpec((1, KV_RANK, ATTN_HEADS * QB), lambda bi, qi: (bi, 0, qi)),
                pl.BlockSpec((1, SUBLANES, QB), lambda bi, qi: (bi, 0, qi)),
                pl.BlockSpec((QB, ATTN_WIDTH), lambda bi, qi: (bi * nb + qi, C_GA // ATTN_WIDTH)),
                pl.BlockSpec((QB, QB), lambda bi, qi: (0, 0))]
    return pl.pallas_call(
        functools.partial(_dsa_kernel, topk=topk),
        grid=(b, nb),
        in_specs=in_specs,
        out_specs=pl.BlockSpec((QB, ATTN_WIDTH), lambda bi, qi: (bi * nb + qi, 0)),
        out_shape=jax.ShapeDtypeStruct((b * t, ATTN_WIDTH), F32),
        scratch_shapes=[pltpu.VMEM((nb, QB, QB), I32),
                        pltpu.VMEM((nb, QB, ATTN_HEADS * QB), F32),
                        pltpu.VMEM((ATTN_WIDTH, QB), F32)],
        compiler_params=_params(("parallel", "arbitrary")),
        name="dsa",
    )(kidx, ckv, vt, qidx, qabs, wit, p, tri)


def _outproj_kernel(yr_ref, ya_ref, x_ref, w_ref, g_ref, o_ref):
    w = w_ref[...]
    y = _dot(yr_ref[...].astype(BF16), w[0:RWKV_WIDTH]) + \
        _dot(ya_ref[...].astype(BF16), w[RWKV_WIDTH:])
    z = x_ref[...] + y
    ms = jnp.mean(z * z, axis=-1, keepdims=True)
    o_ref[...] = z * lax.rsqrt(ms + NORM_EPS) * g_ref[...]


def _outproj(yr, ya, x2, w_out, final_g, tm):
    n, d = x2.shape
    return pl.pallas_call(
        _outproj_kernel,
        grid=(n // tm,),
        in_specs=[pl.BlockSpec((tm, RWKV_WIDTH), lambda i: (i, 0)),
                  pl.BlockSpec((tm, ATTN_WIDTH), lambda i: (i, 0)),
                  pl.BlockSpec((tm, d), lambda i: (i, 0)),
                  pl.BlockSpec(w_out.shape, lambda i: (0, 0)),
                  pl.BlockSpec((1, d), lambda i: (0, 0))],
        out_specs=pl.BlockSpec((tm, d), lambda i: (i, 0)),
        out_shape=jax.ShapeDtypeStruct((n, d), F32),
        compiler_params=_params(("parallel",)),
        name="outproj",
    )(yr, ya, x2, w_out, final_g)


def _pad_rows(w, start, total):
    return jnp.zeros((total, w.shape[1]), w.dtype).at[start:start + w.shape[0]].set(w)


def _layer(x2, b, t, norm_g, w_in, mu_shift, w0, w_up, a0, a_up, k_k, k_a, r_k, gn_g, gn_b,
           q_norm_g, kv_norm_g, w_uq, w_uk, w_uv, w_qidx, kidx_g, kidx_b, w_out, final_g):
    W = RWKV_WIDTH
    d = x2.shape[1]
    o_r, o_k, o_v = 0, W, 2 * W
    o_wd = 3 * W
    o_ad = o_wd + DECAY_LORA
    o_gr = o_ad + AAA_LORA
    o_qd = o_gr + W
    o_kv = o_qd + Q_RANK
    o_ki = o_kv + KV_RANK
    o_wi = o_ki + IDX_DIM
    o_ga = o_wi + IDX_HEADS
    cols = lambda s, n: w_in[:, s:s + n]
    w_all = jnp.concatenate(
        [cols(o_r, 3 * W), cols(o_gr, W), cols(o_ga, ATTN_WIDTH), cols(o_qd, Q_RANK),
         cols(o_kv, KV_RANK), cols(o_wd, DECAY_LORA), cols(o_ad, AAA_LORA), cols(o_ki, IDX_DIM),
         cols(o_wi, IDX_HEADS), jnp.zeros((d, LANES - IDX_HEADS), w_in.dtype)],
        axis=1).astype(BF16)

    row2 = lambda v: v.reshape(1, -1).astype(F32)
    mu_rkv = row2(mu_shift[0:3 * W])
    mu_sm = row2(jnp.concatenate([mu_shift[3 * W:], jnp.zeros((IDX_DIM,), F32)]))
    wup_pad = _pad_rows(w_up, 0, LANES).astype(BF16)
    aup_pad = _pad_rows(a_up, DECAY_LORA, LANES).astype(BF16)
    hid = np.arange(W) // HEAD_DIM
    bd = jnp.asarray((hid[:, None] == hid[None, :]).astype(np.float32), dtype=BF16)

    tb_r = min(256, t)
    ti = np.arange(tb_r)
    tri_r = jnp.asarray(((ti[:, None] // CHUNK == ti[None, :] // CHUNK) &
                         (ti[None, :] <= ti[:, None])).astype(np.float32), dtype=BF16)
    rwkv_consts = [mu_rkv, mu_sm, row2(w0), wup_pad, row2(a0), aup_pad, row2(k_k), row2(k_a),
                   row2(r_k), row2(gn_g), row2(gn_b), bd, tri_r]

    wq_t = jnp.transpose(w_qidx.reshape(Q_RANK, IDX_HEADS, IDX_DIM), (1, 2, 0))
    wq_t = jnp.concatenate([jnp.zeros_like(wq_t), wq_t], axis=1).astype(BF16)
    w_uq_heads = jnp.transpose(w_uq.reshape(Q_RANK, ATTN_HEADS, HEAD_DIM), (1, 0, 2))
    wabs_t = _absorb(w_uk, w_uq_heads)
    wuv_t = jnp.transpose(w_uv, (0, 2, 1)).reshape(ATTN_WIDTH, KV_RANK).astype(BF16)
    lng = row2(jnp.concatenate([jnp.zeros((IDX_DIM,), F32), kidx_g]))
    lnb = row2(jnp.concatenate([jnp.zeros((IDX_DIM,), F32), kidx_b]))
    prep_consts = [row2(q_norm_g), row2(kv_norm_g), lng, lnb, wq_t, wabs_t, wuv_t]
    ki = np.arange(QB)
    tri_q = jnp.asarray((ki[None, :] < ki[:, None]).astype(np.float32), dtype=BF16)

    tm = min(256, b * t)
    p = _inproj(x2, row2(norm_g), w_all, tm)
    y_r = _rwkv(p, b, t, tb_r, rwkv_consts)
    prep = _dsaprep(p, b, t, min(256, t), prep_consts)
    y_a = _dsa(p, prep, tri_q, b, t, min(TOPK_MAX, t // 4))
    return _outproj(y_r, y_a, x2, w_out.astype(BF16), row2(final_g), tm)


def kernel(x, norm_g, w_in, mu_shift, w0, w_up, a0, a_up, k_k, k_a, r_k, gn_g, gn_b,
           q_norm_g, kv_norm_g, w_uq, w_uk, w_uv, w_qidx, kidx_g, kidx_b, w_out, final_g):
    b, t, d = x.shape
    assert norm_g.shape[0] == 1, "single-layer problem"
    out = _layer(x.reshape(b * t, d), b, t, norm_g[0], w_in[0], mu_shift[0], w0[0], w_up[0],
                 a0[0], a_up[0], k_k[0], k_a[0], r_k[0], gn_g[0], gn_b[0], q_norm_g[0],
                 kv_norm_g[0], w_uq[0], w_uk[0], w_uv[0], w_qidx[0], kidx_g[0], kidx_b[0],
                 w_out[0], final_g)
    return out.reshape(b, t, d)
```

```python
import functools

import numpy as np
import jax
import jax.numpy as jnp
from jax import lax
from jax.experimental import pallas as pl
from jax.experimental.pallas import tpu as pltpu

F32 = jnp.float32
BF16 = jnp.bfloat16
I32 = jnp.int32

HEAD_DIM = 64
RWKV_HEADS = 8
RWKV_WIDTH = RWKV_HEADS * HEAD_DIM
DECAY_LORA = 32
AAA_LORA = 32
ATTN_HEADS = 8
ATTN_WIDTH = ATTN_HEADS * HEAD_DIM
Q_RANK = 256
KV_RANK = 128
IDX_HEADS = 4
IDX_DIM = 64
TOPK_MAX = 256
NORM_EPS = 1e-6
LN_EPS = 1e-5
GN_EPS = 64e-5

LANES = 128
SUBLANES = 8
QB = 256
CHUNK = 64
PAIR = 2 * HEAD_DIM

C_RKV = 0
C_GR = 3 * RWKV_WIDTH
C_GA = C_GR + RWKV_WIDTH
C_QD = C_GA + ATTN_WIDTH
C_KV = C_QD + Q_RANK
C_SM = C_KV + KV_RANK
C_WI = C_SM + LANES
P_COLS = C_WI + LANES

VMEM_LIMIT = 56 * 1024 * 1024


def _params(sem):
    return pltpu.CompilerParams(dimension_semantics=sem, vmem_limit_bytes=VMEM_LIMIT)


def _dot(a, b):
    return jnp.dot(a, b, preferred_element_type=F32)


def _dot_nt(a, b):
    return lax.dot_general(a, b, (((1,), (1,)), ((), ())), preferred_element_type=F32)


def _dot_tn(a, b):
    return lax.dot_general(a, b, (((0,), (0,)), ((), ())), preferred_element_type=F32)


def _split2(x):
    hi = x.astype(BF16)
    lo = (x - hi.astype(F32)).astype(BF16)
    return hi, lo


def _split3(x):
    hi = x.astype(BF16)
    r1 = x - hi.astype(F32)
    mid = r1.astype(BF16)
    lo = (r1 - mid.astype(F32)).astype(BF16)
    return hi, mid, lo


def _dot_exact_rhs(x, m_bf16, parts=2):
    if parts == 2:
        hi, lo = _split2(x)
        return _dot(hi, m_bf16) + _dot(lo, m_bf16)
    hi, mid, lo = _split3(x)
    return _dot(hi, m_bf16) + _dot(mid, m_bf16) + _dot(lo, m_bf16)


def _fold8(x, op):
    acc = x[0:SUBLANES]
    for j in range(1, x.shape[0] // SUBLANES):
        acc = op(acc, x[j * SUBLANES:(j + 1) * SUBLANES])
    return acc


def _colsum(x):
    return jnp.sum(_fold8(x, jnp.add), axis=0, keepdims=True)


def _colmax(x):
    return jnp.max(_fold8(x, jnp.maximum), axis=0, keepdims=True)


def _sigmoid(x):
    return 1.0 / (1.0 + jnp.exp(-x))


def _absorb_kernel(wuk_ref, wuq_ref, o_ref):
    a = wuk_ref[0]
    b = wuq_ref[0]
    ah, al = _split2(a)
    bh, bl = _split2(b)
    acc = _dot_nt(ah, bh) + _dot_nt(ah, bl) + _dot_nt(al, bh)
    o_ref[0] = acc * np.float32(HEAD_DIM ** -0.5 * np.log2(np.e))


def _absorb(w_uk, w_uq_heads):
    return pl.pallas_call(
        _absorb_kernel,
        grid=(ATTN_HEADS,),
        in_specs=[pl.BlockSpec((1, KV_RANK, HEAD_DIM), lambda h: (h, 0, 0)),
                  pl.BlockSpec((1, Q_RANK, HEAD_DIM), lambda h: (h, 0, 0))],
        out_specs=pl.BlockSpec((1, KV_RANK, Q_RANK), lambda h: (h, 0, 0)),
        out_shape=jax.ShapeDtypeStruct((ATTN_HEADS, KV_RANK, Q_RANK), F32),
        compiler_params=_params(("arbitrary",)),
        name="absorb",
    )(w_uk, w_uq_heads)


def _inproj_kernel(x_ref, g_ref, w_ref, o_ref):
    x = x_ref[...]
    ms = jnp.mean(x * x, axis=-1, keepdims=True)
    xn = x * lax.rsqrt(ms + NORM_EPS) * g_ref[...]
    o_ref[...] = _dot(xn.astype(BF16), w_ref[...])


def _inproj(x2, norm_g, w_all, tm):
    n, d = x2.shape
    return pl.pallas_call(
        _inproj_kernel,
        grid=(n // tm,),
        in_specs=[pl.BlockSpec((tm, d), lambda i: (i, 0)),
                  pl.BlockSpec((1, d), lambda i: (0, 0)),
                  pl.BlockSpec((d, P_COLS), lambda i: (0, 0))],
        out_specs=pl.BlockSpec((tm, P_COLS), lambda i: (i, 0)),
        out_shape=jax.ShapeDtypeStruct((n, P_COLS), F32),
        compiler_params=_params(("parallel",)),
        name="inproj",
    )(x2, norm_g, w_all)


def _rwkv_kernel(rkv_ref, sm_ref, gate_ref, mu_rkv_ref, mu_sm_ref, w0_ref, wup_ref, a0_ref,
                 aup_ref, kkw_ref, ka_ref, rk_ref, gng_ref, gnb_ref, bd_ref, tri_ref,
                 o_ref,
                 carry_rkv, carry_sm, s_scr, n_scr, r_scr, u_scr, k_scr, v_scr, wt_scr, y_scr,
                 *, tb):
    t = pl.program_id(1)
    W = RWKV_WIDTH

    @pl.when(t == 0)
    def _():
        carry_rkv[...] = jnp.zeros_like(carry_rkv)
        carry_sm[...] = jnp.zeros_like(carry_sm)
        s_scr[...] = jnp.zeros_like(s_scr)

    p = rkv_ref[...]
    row = lax.broadcasted_iota(I32, (tb, 1), 0)
    prev = jnp.where(row == 0, carry_rkv[0:1, :], pltpu.roll(p, 1, 0))
    carry_rkv[0:1, :] = p[tb - 1:tb, :]
    xs = p + (prev - p) * mu_rkv_ref[...]
    r = xs[:, 0:W]
    k = xs[:, W:2 * W]
    v = xs[:, 2 * W:3 * W]

    ps = sm_ref[...]
    prev_s = jnp.where(row == 0, carry_sm[0:1, :], pltpu.roll(ps, 1, 0))
    carry_sm[0:1, :] = ps[tb - 1:tb, :]
    xl = ps + (prev_s - ps) * mu_sm_ref[...]

    zw = w0_ref[...] + _dot(jnp.tanh(xl).astype(BF16), wup_ref[...])
    logw = -np.float32(np.exp(-0.5)) * _sigmoid(zw)
    a = _sigmoid(a0_ref[...] + _dot(xl.astype(BF16), aup_ref[...]))

    bd = bd_ref[...]
    kk = k * kkw_ref[...]
    ss = _dot_exact_rhs(kk * kk, bd)
    kk = kk * (1.0 / jnp.maximum(jnp.sqrt(ss), 1e-12))
    k2 = k * (1.0 + (a - 1.0) * ka_ref[...])
    bonus = _dot_exact_rhs(r * k2 * rk_ref[...], bd)

    hi, mid, lo = _split3(logw)
    tri = tri_ref[...]
    cum = _dot(tri, hi) + _dot(tri, mid) + _dot(tri, lo)
    wt = jnp.exp(cum)
    wi = jnp.exp(-cum)
    wx = jnp.exp(cum - logw)
    n_scr[...] = (-kk * wx).astype(BF16)
    r_scr[...] = (r * wt).astype(BF16)
    u_scr[...] = (kk * a * wi).astype(BF16)
    k_scr[...] = (k2 * wi).astype(BF16)
    v_scr[...] = v.astype(BF16)
    wt_scr[...] = wt

    rowp = lax.broadcasted_iota(I32, (PAIR, PAIR), 0)
    colp = lax.broadcasted_iota(I32, (PAIR, PAIR), 1)
    same_blk = (rowp >= HEAD_DIM) == (colp >= HEAD_DIM)
    strict_lo = (colp % HEAD_DIM) < (rowp % HEAD_DIM)
    m_abd = same_blk & strict_lo
    m_ank = jnp.logical_not(same_blk) & strict_lo
    rowc = lax.broadcasted_iota(I32, (CHUNK, PAIR), 0)
    colc = lax.broadcasted_iota(I32, (CHUNK, PAIR), 1)
    incl_lo = (colc % HEAD_DIM) <= rowc
    head0_c = colc < HEAD_DIM
    head0_p = colp < HEAD_DIM

    def chunk_body(c, carry):
        off = pl.multiple_of(c * CHUNK, CHUNK)
        last8 = pl.multiple_of(c * CHUNK + CHUNK - SUBLANES, SUBLANES)
        for g in range(RWKV_HEADS // 2):
            sl = slice(g * PAIR, (g + 1) * PAIR)
            nn = n_scr[pl.ds(off, CHUNK), sl]
            rr = r_scr[pl.ds(off, CHUNK), sl]
            uu = u_scr[pl.ds(off, CHUNK), sl]
            kc = k_scr[pl.ds(off, CHUNK), sl]
            vv = v_scr[pl.ds(off, CHUNK), sl]
            wc = wt_scr[pl.ds(last8, SUBLANES), sl][SUBLANES - 1:SUBLANES, :]

            nr = jnp.concatenate([nn, rr], axis=0)
            zero = jnp.zeros_like(nr)
            nr0 = jnp.where(head0_p, nr, zero)
            nr1 = jnp.where(head0_p, zero, nr)
            uk = jnp.concatenate([uu, kc], axis=0)
            ku = jnp.concatenate([kc, uu], axis=0)
            g0 = _dot_nt(nr0, uk)
            g1 = _dot_nt(nr1, ku)
            s_old = s_scr[g]
            nrh = _dot_nt(nr, s_old.astype(BF16))
            nh = nrh[0:CHUNK]
            rh = nrh[CHUNK:PAIR]

            gtop = jnp.concatenate([g0[0:CHUNK], g1[0:CHUNK]], axis=0)
            abd = jnp.where(m_abd, gtop, 0.0)
            ank = jnp.where(m_ank, gtop, 0.0)
            vvv = jnp.concatenate([vv, vv], axis=0)
            zf = jnp.concatenate([nh, nh], axis=0) + _dot(ank.astype(BF16), vvv)
            pw = abd
            for it in range(6):
                pwb = pw.astype(BF16)
                zf = zf + _dot(pwb, zf.astype(BF16))
                if it < 5:
                    pw = _dot(pwb, pwb)
            z = jnp.where(head0_c, zf[0:CHUNK], zf[CHUNK:PAIR])
            zb = z.astype(BF16)
            zv = jnp.concatenate([zb, vv], axis=0)
            vz = jnp.concatenate([vv, zb], axis=0)
            ar0 = jnp.where(incl_lo, g0[CHUNK:PAIR], 0.0).astype(BF16)
            ar1 = jnp.where(incl_lo, g1[CHUNK:PAIR], 0.0).astype(BF16)
            y = rh + jnp.where(head0_c, _dot(ar0, zv), _dot(ar1, vz))
            y_scr[pl.ds(off, CHUNK), sl] = y
            s_new = (s_old + _dot_tn(zv, uk)) * wc
            s_scr[g] = jnp.where(same_blk, s_new, 0.0)
        return carry

    lax.fori_loop(0, tb // CHUNK, chunk_body, 0)

    y = y_scr[...]
    inv_d = np.float32(1.0 / HEAD_DIM)
    mu = _dot_exact_rhs(y, bd) * inv_d
    yc = y - mu
    var = _dot_exact_rhs(yc * yc, bd) * inv_d
    y = yc * lax.rsqrt(var + GN_EPS) * gng_ref[...] + gnb_ref[...]
    y = y + bonus * v
    gt = gate_ref[...]
    o_ref[...] = y * (gt * _sigmoid(gt))


def _rwkv(p, b, t, tb, consts):
    nt = t // tb
    W = RWKV_WIDTH
    row_spec = lambda width, cb: pl.BlockSpec((tb, width), lambda bi, ti: (bi * nt + ti, cb))
    full = lambda arr: pl.BlockSpec(arr.shape, lambda bi, ti: (0,) * arr.ndim)
    in_specs = [row_spec(3 * W, C_RKV // (3 * W)), row_spec(LANES, C_SM // LANES),
                row_spec(W, C_GR // W)] + [full(c) for c in consts]
    return pl.pallas_call(
        functools.partial(_rwkv_kernel, tb=tb),
        grid=(b, nt),
        in_specs=in_specs,
        out_specs=pl.BlockSpec((tb, W), lambda bi, ti: (bi * nt + ti, 0)),
        out_shape=jax.ShapeDtypeStruct((b * t, W), F32),
        scratch_shapes=[pltpu.VMEM((SUBLANES, 3 * W), F32), pltpu.VMEM((SUBLANES, LANES), F32),
                        pltpu.VMEM((RWKV_HEADS // 2, PAIR, PAIR), F32),
                        pltpu.VMEM((tb, W), BF16), pltpu.VMEM((tb, W), BF16),
                        pltpu.VMEM((tb, W), BF16), pltpu.VMEM((tb, W), BF16),
                        pltpu.VMEM((tb, W), BF16), pltpu.VMEM((tb, W), F32),
                        pltpu.VMEM((tb, W), F32)],
        compiler_params=_params(("parallel", "arbitrary")),
        name="rwkv",
    )(p, p, p, *consts)


def _dsaprep_kernel(qd_ref, kv_ref, sm_ref, wi_ref, qg_ref, kvg_ref, lng_ref, lnb_ref,
                    wqidx_ref, wabs_ref, wuv_ref,
                    ckv_ref, kidx_ref, vt_ref, qidx_ref, qabs_ref, wit_ref, *, tb):
    nq = tb // QB
    qd = qd_ref[...]
    cq = qd * lax.rsqrt(jnp.mean(qd * qd, axis=-1, keepdims=True) + NORM_EPS) * qg_ref[...]
    cqb = cq.astype(BF16)
    kv = kv_ref[...]
    ckv = kv * lax.rsqrt(jnp.mean(kv * kv, axis=-1, keepdims=True) + NORM_EPS) * kvg_ref[...]
    ckvb = ckv.astype(BF16)

    sm = sm_ref[...]
    lane = lax.broadcasted_iota(I32, sm.shape, 1)
    hi_half = lane >= IDX_DIM
    inv_d = np.float32(1.0 / IDX_DIM)
    mu = jnp.sum(jnp.where(hi_half, sm, 0.0), axis=-1, keepdims=True) * inv_d
    xc = jnp.where(hi_half, sm - mu, 0.0)
    var = jnp.sum(xc * xc, axis=-1, keepdims=True) * inv_d
    kidx = (xc * lax.rsqrt(var + LN_EPS) * lng_ref[...] + lnb_ref[...]).astype(BF16)

    vt = _dot_nt(wuv_ref[...], ckvb).astype(BF16)
    wit = jnp.transpose(wi_ref[...])[0:SUBLANES, :] * np.float32(IDX_HEADS ** -0.5)
    for j in range(nq):
        qs = slice(j * QB, (j + 1) * QB)
        ckv_ref[0, j] = ckvb[qs, :]
        kidx_ref[0, j] = kidx[qs, :]
        vt_ref[0, j] = vt[:, qs]
    wit_ref[0] = wit
    for h in range(IDX_HEADS):
        res = _dot_nt(wqidx_ref[h], cqb).astype(BF16)
        for j in range(nq):
            qidx_ref[0, :, (j * IDX_HEADS + h) * QB:(j * IDX_HEADS + h + 1) * QB] = \
                res[:, j * QB:(j + 1) * QB]
    for h in range(ATTN_HEADS):
        res = _dot_nt(wabs_ref[h].astype(BF16), cqb).astype(BF16)
        for j in range(nq):
            qabs_ref[0, :, (j * ATTN_HEADS + h) * QB:(j * ATTN_HEADS + h + 1) * QB] = \
                res[:, j * QB:(j + 1) * QB]


def _dsaprep(p, b, t, tb, consts):
    nt = t // tb
    nb = t // QB
    nq = tb // QB
    row_spec = lambda width, cb: pl.BlockSpec((tb, width), lambda bi, ti: (bi * nt + ti, cb))
    full = lambda arr: pl.BlockSpec(arr.shape, lambda bi, ti: (0,) * arr.ndim)
    in_specs = [row_spec(Q_RANK, C_QD // Q_RANK), row_spec(KV_RANK, C_KV // KV_RANK),
                row_spec(LANES, C_SM // LANES), row_spec(LANES, C_WI // LANES)] + \
               [full(c) for c in consts]
    out_shape = [jax.ShapeDtypeStruct((b, nb, QB, KV_RANK), BF16),
                 jax.ShapeDtypeStruct((b, nb, QB, LANES), BF16),
                 jax.ShapeDtypeStruct((b, nb, ATTN_WIDTH, QB), BF16),
                 jax.ShapeDtypeStruct((b, LANES, nb * IDX_HEADS * QB), BF16),
                 jax.ShapeDtypeStruct((b, KV_RANK, nb * ATTN_HEADS * QB), BF16),
                 jax.ShapeDtypeStruct((b, SUBLANES, t), F32)]
    out_specs = [pl.BlockSpec((1, nq, QB, KV_RANK), lambda bi, ti: (bi, ti, 0, 0)),
                 pl.BlockSpec((1, nq, QB, LANES), lambda bi, ti: (bi, ti, 0, 0)),
                 pl.BlockSpec((1, nq, ATTN_WIDTH, QB), lambda bi, ti: (bi, ti, 0, 0)),
                 pl.BlockSpec((1, LANES, nq * IDX_HEADS * QB), lambda bi, ti: (bi, 0, ti)),
                 pl.BlockSpec((1, KV_RANK, nq * ATTN_HEADS * QB), lambda bi, ti: (bi, 0, ti)),
                 pl.BlockSpec((1, SUBLANES, tb), lambda bi, ti: (bi, 0, ti))]
    return pl.pallas_call(
        functools.partial(_dsaprep_kernel, tb=tb),
        grid=(b, nt),
        in_specs=in_specs,
        out_specs=out_specs,
        out_shape=out_shape,
        compiler_params=_params(("parallel", "parallel")),
        name="dsaprep",
    )(p, p, p, p, *consts)


INT_MIN = np.int32(-2 ** 31)


def _dsa_kernel(kidx_ref, ckv_ref, vt_ref, qidx_ref, qabs_ref, wi_ref, gate_ref, tri_ref,
                o_ref, key_scr, s_scr, acc_scr, *, topk):
    i = pl.program_id(1)
    nch = i + 1
    row = lax.broadcasted_iota(I32, (QB, QB), 0)
    col = lax.broadcasted_iota(I32, (QB, QB), 1)
    qpos = i * QB + col
    wi = wi_ref[0]

    def score_body(c, carry):
        kc = kidx_ref[0, c]
        sc = jnp.zeros((QB, QB), F32)
        for h in range(IDX_HEADS):
            lg = _dot(kc, qidx_ref[0, :, h * QB:(h + 1) * QB])
            sc = sc + wi[h:h + 1, :] * jnp.maximum(lg, 0.0)
        sc = jnp.where(c * QB + row <= qpos, sc, -jnp.inf)
        bits = lax.bitcast_convert_type(sc, I32)
        key = jnp.where(bits < 0, bits ^ np.int32(0x7FFFFFFF), bits)
        key_scr[c] = jnp.where(sc == 0.0, 0, key)
        return carry

    lax.fori_loop(0, nch, score_body, 0)

    def count(pred):
        def body(c, acc):
            return acc + _fold8(pred(key_scr[c]).astype(I32), jnp.add)
        acc = lax.fori_loop(0, nch, body, jnp.zeros((SUBLANES, QB), I32))
        return jnp.sum(acc, axis=0, keepdims=True)

    thr = jnp.full((1, QB), INT_MIN, I32)
    n_gt = jnp.zeros((1, QB), I32)
    for bit in range(31, -1, -1):
        cand = jnp.zeros((1, QB), I32) if bit == 31 else thr | np.int32(1 << bit)
        cnt = count(lambda k, cand=cand: k >= cand)
        ok = cnt >= topk
        thr = jnp.where(ok, cand, thr)
        n_gt = jnp.where(ok, n_gt, cnt)
    need = (topk - n_gt).astype(F32)

    tri = tri_ref[...]

    def pass1(c, carry):
        run, ms = carry
        key = key_scr[c]
        eq = key == thr
        eqf = eq.astype(F32)
        before = _dot(tri, eqf.astype(BF16)) + run
        sel = (key > thr) | (eq & (before < need))
        sel = sel & (c * QB + row <= qpos)
        run = run + _colsum(eqf)
        s = _dot(ckv_ref[0, c], qabs_ref[0])
        new_ms = []
        for h in range(ATTN_HEADS):
            sh = jnp.where(sel, s[:, h * QB:(h + 1) * QB], -jnp.inf)
            s_scr[c, :, h * QB:(h + 1) * QB] = sh
            new_ms.append(jnp.maximum(ms[h], _colmax(sh)))
        return run, tuple(new_ms)

    ninf = jnp.full((1, QB), -jnp.inf, F32)
    _, ms = lax.fori_loop(0, nch, pass1,
                          (jnp.zeros((1, QB), F32), tuple(ninf for _ in range(ATTN_HEADS))))

    acc_scr[...] = jnp.zeros_like(acc_scr)

    def pass2(c, ls):
        new_ls = []
        for h in range(ATTN_HEADS):
            p = jnp.exp2(s_scr[c, :, h * QB:(h + 1) * QB] - ms[h])
            new_ls.append(ls[h] + _colsum(p))
            hs = slice(h * HEAD_DIM, (h + 1) * HEAD_DIM)
            acc_scr[hs, :] += _dot(vt_ref[0, c, hs, :], p.astype(BF16))
        return tuple(new_ls)

    zero = jnp.zeros((1, QB), F32)
    ls = lax.fori_loop(0, nch, pass2, tuple(zero for _ in range(ATTN_HEADS)))

    for h in range(ATTN_HEADS):
        hs = slice(h * HEAD_DIM, (h + 1) * HEAD_DIM)
        acc_scr[hs, :] = acc_scr[hs, :] * (1.0 / ls[h])
    o = jnp.transpose(acc_scr[...])
    gt = gate_ref[...]
    o_ref[...] = o * (gt * _sigmoid(gt))


def _dsa(p, prep, tri, b, t, topk):
    nb = t // QB
    ckv, kidx, vt, qidx, qabs, wit = prep
    in_specs = [pl.BlockSpec((1, nb, QB, LANES), lambda bi, qi: (bi, 0, 0, 0)),
                pl.BlockSpec((1, nb, QB, KV_RANK), lambda bi, qi: (bi, 0, 0, 0)),
                pl.BlockSpec((1, nb, ATTN_WIDTH, QB), lambda bi, qi: (bi, 0, 0, 0)),
                pl.BlockSpec((1, LANES, IDX_HEADS * QB), lambda bi, qi: (bi, 0, qi)),
                pl.BlockSpec((1, KV_RANK, ATTN_HEADS * QB), lambda bi, qi: (bi, 0, qi)),
                pl.BlockSpec((1, SUBLANES, QB), lambda bi, qi: (bi, 0, qi)),
                pl.BlockSpec((QB, ATTN_WIDTH), lambda bi, qi: (bi * nb + qi, C_GA // ATTN_WIDTH)),
                pl.BlockSpec((QB, QB), lambda bi, qi: (0, 0))]
    return pl.pallas_call(
        functools.partial(_dsa_kernel, topk=topk),
        grid=(b, nb),
        in_specs=in_specs,
        out_specs=pl.BlockSpec((QB, ATTN_WIDTH), lambda bi, qi: (bi * nb + qi, 0)),
        out_shape=jax.ShapeDtypeStruct((b * t, ATTN_WIDTH), F32),
        scratch_shapes=[pltpu.VMEM((nb, QB, QB), I32),
                        pltpu.VMEM((nb, QB, ATTN_HEADS * QB), F32),
                        pltpu.VMEM((ATTN_WIDTH, QB), F32)],
        compiler_params=_params(("parallel", "arbitrary")),
        name="dsa",
    )(kidx, ckv, vt, qidx, qabs, wit, p, tri)


def _outproj_kernel(yr_ref, ya_ref, x_ref, w_ref, g_ref, o_ref):
    w = w_ref[...]
    y = _dot(yr_ref[...].astype(BF16), w[0:RWKV_WIDTH]) + \
        _dot(ya_ref[...].astype(BF16), w[RWKV_WIDTH:])
    z = x_ref[...] + y
    ms = jnp.mean(z * z, axis=-1, keepdims=True)
    o_ref[...] = z * lax.rsqrt(ms + NORM_EPS) * g_ref[...]


def _outproj(yr, ya, x2, w_out, final_g, tm):
    n, d = x2.shape
    return pl.pallas_call(
        _outproj_kernel,
        grid=(n // tm,),
        in_specs=[pl.BlockSpec((tm, RWKV_WIDTH), lambda i: (i, 0)),
                  pl.BlockSpec((tm, ATTN_WIDTH), lambda i: (i, 0)),
                  pl.BlockSpec((tm, d), lambda i: (i, 0)),
                  pl.BlockSpec(w_out.shape, lambda i: (0, 0)),
                  pl.BlockSpec((1, d), lambda i: (0, 0))],
        out_specs=pl.BlockSpec((tm, d), lambda i: (i, 0)),
        out_shape=jax.ShapeDtypeStruct((n, d), F32),
        compiler_params=_params(("parallel",)),
        name="outproj",
    )(yr, ya, x2, w_out, final_g)


def _pad_rows(w, start, total):
    return jnp.zeros((total, w.shape[1]), w.dtype).at[start:start + w.shape[0]].set(w)


def _layer(x2, b, t, norm_g, w_in, mu_shift, w0, w_up, a0, a_up, k_k, k_a, r_k, gn_g, gn_b,
           q_norm_g, kv_norm_g, w_uq, w_uk, w_uv, w_qidx, kidx_g, kidx_b, w_out, final_g):
    W = RWKV_WIDTH
    d = x2.shape[1]
    o_r, o_k, o_v = 0, W, 2 * W
    o_wd = 3 * W
    o_ad = o_wd + DECAY_LORA
    o_gr = o_ad + AAA_LORA
    o_qd = o_gr + W
    o_kv = o_qd + Q_RANK
    o_ki = o_kv + KV_RANK
    o_wi = o_ki + IDX_DIM
    o_ga = o_wi + IDX_HEADS
    cols = lambda s, n: w_in[:, s:s + n]
    w_all = jnp.concatenate(
        [cols(o_r, 3 * W), cols(o_gr, W), cols(o_ga, ATTN_WIDTH), cols(o_qd, Q_RANK),
         cols(o_kv, KV_RANK), cols(o_wd, DECAY_LORA), cols(o_ad, AAA_LORA), cols(o_ki, IDX_DIM),
         cols(o_wi, IDX_HEADS), jnp.zeros((d, LANES - IDX_HEADS), w_in.dtype)],
        axis=1).astype(BF16)

    row2 = lambda v: v.reshape(1, -1).astype(F32)
    mu_rkv = row2(mu_shift[0:3 * W])
    mu_sm = row2(jnp.concatenate([mu_shift[3 * W:], jnp.zeros((IDX_DIM,), F32)]))
    wup_pad = _pad_rows(w_up, 0, LANES).astype(BF16)
    aup_pad = _pad_rows(a_up, DECAY_LORA, LANES).astype(BF16)
    hid = np.arange(W) // HEAD_DIM
    bd = jnp.asarray((hid[:, None] == hid[None, :]).astype(np.float32), dtype=BF16)

    tb_r = min(256, t)
    ti = np.arange(tb_r)
    tri_r = jnp.asarray(((ti[:, None] // CHUNK == ti[None, :] // CHUNK) &
                         (ti[None, :] <= ti[:, None])).astype(np.float32), dtype=BF16)
    rwkv_consts = [mu_rkv, mu_sm, row2(w0), wup_pad, row2(a0), aup_pad, row2(k_k), row2(k_a),
                   row2(r_k), row2(gn_g), row2(gn_b), bd, tri_r]

    wq_t = jnp.transpose(w_qidx.reshape(Q_RANK, IDX_HEADS, IDX_DIM), (1, 2, 0))
    wq_t = jnp.concatenate([jnp.zeros_like(wq_t), wq_t], axis=1).astype(BF16)
    w_uq_heads = jnp.transpose(w_uq.reshape(Q_RANK, ATTN_HEADS, HEAD_DIM), (1, 0, 2))
    wabs_t = _absorb(w_uk, w_uq_heads)
    wuv_t = jnp.transpose(w_uv, (0, 2, 1)).reshape(ATTN_WIDTH, KV_RANK).astype(BF16)
    lng = row2(jnp.concatenate([jnp.zeros((IDX_DIM,), F32), kidx_g]))
    lnb = row2(jnp.concatenate([jnp.zeros((IDX_DIM,), F32), kidx_b]))
    prep_consts = [row2(q_norm_g), row2(kv_norm_g), lng, lnb, wq_t, wabs_t, wuv_t]
    ki = np.arange(QB)
    tri_q = jnp.asarray((ki[None, :] < ki[:, None]).astype(np.float32), dtype=BF16)

    tm = min(256, b * t)
    p = _inproj(x2, row2(norm_g), w_all, tm)
    y_r = _rwkv(p, b, t, tb_r, rwkv_consts)
    prep = _dsaprep(p, b, t, min(256, t), prep_consts)
    y_a = _dsa(p, prep, tri_q, b, t, min(TOPK_MAX, t // 4))
    return _outproj(y_r, y_a, x2, w_out.astype(BF16), row2(final_g), tm)


def kernel(x, norm_g, w_in, mu_shift, w0, w_up, a0, a_up, k_k, k_a, r_k, gn_g, gn_b,
           q_norm_g, kv_norm_g, w_uq, w_uk, w_uv, w_qidx, kidx_g, kidx_b, w_out, final_g):
    b, t, d = x.shape
    assert norm_g.shape[0] == 1, "single-layer problem"
    out = _layer(x.reshape(b * t, d), b, t, norm_g[0], w_in[0], mu_shift[0], w0[0], w_up[0],
                 a0[0], a_up[0], k_k[0], k_a[0], r_k[0], gn_g[0], gn_b[0], q_norm_g[0],
                 kv_norm_g[0], w_uq[0], w_uk[0], w_uv[0], w_qidx[0], kidx_g[0], kidx_b[0],
                 w_out[0], final_g)
    return out.reshape(b, t, d)
```

```python
import functools

import numpy as np
import jax
import jax.numpy as jnp
from jax import lax
from jax.experimental import pallas as pl
from jax.experimental.pallas import tpu as pltpu

F32 = jnp.float32
BF16 = jnp.bfloat16
I32 = jnp.int32

HEAD_DIM = 64
RWKV_HEADS = 8
RWKV_WIDTH = RWKV_HEADS * HEAD_DIM
DECAY_LORA = 32
AAA_LORA = 32
ATTN_HEADS = 8
ATTN_WIDTH = ATTN_HEADS * HEAD_DIM
Q_RANK = 256
KV_RANK = 128
IDX_HEADS = 4
IDX_DIM = 64
TOPK_MAX = 256
NORM_EPS = 1e-6
LN_EPS = 1e-5
GN_EPS = 64e-5

LANES = 128
SUBLANES = 8
QB = 256
CHUNK = 64
PAIR = 2 * HEAD_DIM
A_GROUP = 2
VROWS = HEAD_DIM + 16

C_RKV = 0
C_GR = 3 * RWKV_WIDTH
C_GA = C_GR + RWKV_WIDTH
C_QD = C_GA + ATTN_WIDTH
C_KV = C_QD + Q_RANK
C_SM = C_KV + KV_RANK
C_WI = C_SM + LANES
P_COLS = C_WI + LANES

VMEM_LIMIT = 56 * 1024 * 1024


def _params(sem):
    return pltpu.CompilerParams(dimension_semantics=sem, vmem_limit_bytes=VMEM_LIMIT)


def _dot(a, b):
    return jnp.dot(a, b, preferred_element_type=F32)


def _dot_nt(a, b):
    return lax.dot_general(a, b, (((1,), (1,)), ((), ())), preferred_element_type=F32)


def _dot_tn(a, b):
    return lax.dot_general(a, b, (((0,), (0,)), ((), ())), preferred_element_type=F32)


def _split2(x):
    hi = x.astype(BF16)
    lo = (x - hi.astype(F32)).astype(BF16)
    return hi, lo


def _split3(x):
    hi = x.astype(BF16)
    r1 = x - hi.astype(F32)
    mid = r1.astype(BF16)
    lo = (r1 - mid.astype(F32)).astype(BF16)
    return hi, mid, lo


def _dot_exact_rhs(x, m_bf16, parts=2):
    if parts == 2:
        hi, lo = _split2(x)
        return _dot(hi, m_bf16) + _dot(lo, m_bf16)
    hi, mid, lo = _split3(x)
    return _dot(hi, m_bf16) + _dot(mid, m_bf16) + _dot(lo, m_bf16)


def _fold8(x, op):
    acc = x[0:SUBLANES]
    for j in range(1, x.shape[0] // SUBLANES):
        acc = op(acc, x[j * SUBLANES:(j + 1) * SUBLANES])
    return acc


def _colsum(x):
    return jnp.sum(_fold8(x, jnp.add), axis=0, keepdims=True)


def _colmax(x):
    return jnp.max(_fold8(x, jnp.maximum), axis=0, keepdims=True)


def _sigmoid(x):
    return 1.0 / (1.0 + jnp.exp(-x))


def _absorb_kernel(wuk_ref, wuq_ref, o_ref):
    a = wuk_ref[0]
    b = wuq_ref[0]
    ah, al = _split2(a)
    bh, bl = _split2(b)
    acc = _dot_nt(ah, bh) + _dot_nt(ah, bl) + _dot_nt(al, bh)
    o_ref[0] = acc * np.float32(HEAD_DIM ** -0.5 * np.log2(np.e))


def _absorb(w_uk, w_uq_heads):
    return pl.pallas_call(
        _absorb_kernel,
        grid=(ATTN_HEADS,),
        in_specs=[pl.BlockSpec((1, KV_RANK, HEAD_DIM), lambda h: (h, 0, 0)),
                  pl.BlockSpec((1, Q_RANK, HEAD_DIM), lambda h: (h, 0, 0))],
        out_specs=pl.BlockSpec((1, KV_RANK, Q_RANK), lambda h: (h, 0, 0)),
        out_shape=jax.ShapeDtypeStruct((ATTN_HEADS, KV_RANK, Q_RANK), F32),
        compiler_params=_params(("arbitrary",)),
        name="absorb",
    )(w_uk, w_uq_heads)


def _inproj_kernel(x_ref, g_ref, w_ref, o_ref):
    x = x_ref[...]
    ms = jnp.mean(x * x, axis=-1, keepdims=True)
    xn = x * lax.rsqrt(ms + NORM_EPS) * g_ref[...]
    o_ref[...] = _dot(xn.astype(BF16), w_ref[...])


def _inproj(x2, norm_g, w_all, tm):
    n, d = x2.shape
    return pl.pallas_call(
        _inproj_kernel,
        grid=(n // tm,),
        in_specs=[pl.BlockSpec((tm, d), lambda i: (i, 0)),
                  pl.BlockSpec((1, d), lambda i: (0, 0)),
                  pl.BlockSpec((d, P_COLS), lambda i: (0, 0))],
        out_specs=pl.BlockSpec((tm, P_COLS), lambda i: (i, 0)),
        out_shape=jax.ShapeDtypeStruct((n, P_COLS), F32),
        compiler_params=_params(("parallel",)),
        name="inproj",
    )(x2, norm_g, w_all)


def _rwkv_kernel(rkv_ref, sm_ref, gate_ref, mu_rkv_ref, mu_sm_ref, w0_ref, wup_ref, a0_ref,
                 aup_ref, kkw_ref, ka_ref, rk_ref, gng_ref, gnb_ref, bd_ref, tri_ref,
                 o_ref,
                 carry_rkv, carry_sm, s_scr, n_scr, r_scr, u_scr, k_scr, v_scr, wc_scr, bv_scr,
                 t_scr, ankv_scr, ar_scr, *, tb):
    t = pl.program_id(1)
    W = RWKV_WIDTH

    @pl.when(t == 0)
    def _():
        carry_rkv[...] = jnp.zeros_like(carry_rkv)
        carry_sm[...] = jnp.zeros_like(carry_sm)
        s_scr[...] = jnp.zeros_like(s_scr)

    bd = bd_ref[...]
    half = W // 2

    def head_sum(x):
        st = jnp.concatenate([x[:, :half], x[:, half:]], axis=0)
        res = _dot(st.astype(BF16), bd)
        return jnp.concatenate([res[:CHUNK], res[CHUNK:]], axis=1)

    def shifted(ref, carry, c, rows):
        cur = ref[rows, :]
        first = carry[0:1, :] if c == 0 else ref[c * CHUNK - 1:c * CHUNK, :]
        row = lax.broadcasted_iota(I32, (CHUNK, 1), 0)
        return cur, jnp.where(row == 0, first, pltpu.roll(cur, 1, 0))

    def prep(c, rows):
        p, prev = shifted(rkv_ref, carry_rkv, c, rows)
        xs = p + (prev - p) * mu_rkv_ref[...]
        r = xs[:, 0:W]
        k = xs[:, W:2 * W]
        v = xs[:, 2 * W:3 * W]
        ps, prev_s = shifted(sm_ref, carry_sm, c, rows)
        xl = ps + (prev_s - ps) * mu_sm_ref[...]

        zw = w0_ref[...] + _dot(jnp.tanh(xl).astype(BF16), wup_ref[...])
        logw = -np.float32(np.exp(-0.5)) * _sigmoid(zw)
        a = _sigmoid(a0_ref[...] + _dot(xl.astype(BF16), aup_ref[...]))

        kk = k * kkw_ref[...]
        kk = kk * (1.0 / jnp.maximum(jnp.sqrt(head_sum(kk * kk)), 1e-12))
        k2 = k * (1.0 + (a - 1.0) * ka_ref[...])
        bv_scr[rows, :] = head_sum(r * k2 * rk_ref[...]) * v

        hi, mid, lo = _split3(logw)
        tri = tri_ref[...]
        cum = _dot(tri, hi) + _dot(tri, mid) + _dot(tri, lo)
        wt = jnp.exp(cum)
        wi = jnp.exp(-cum)
        wx = jnp.exp(cum - logw)
        nb = (-kk * wx).astype(BF16)
        rb = (r * wt).astype(BF16)
        ub = (kk * a * wi).astype(BF16)
        kb = (k2 * wi).astype(BF16)
        vb = v.astype(BF16)
        n_scr[rows, :] = nb
        r_scr[rows, :] = rb
        u_scr[rows, :] = ub
        k_scr[rows, :] = kb
        v_scr[rows, :] = vb
        wc_scr[c] = wt[CHUNK - SUBLANES:CHUNK, :]
        return nb, rb, ub, kb, vb

    rowp = lax.broadcasted_iota(I32, (PAIR, PAIR), 0)
    colp = lax.broadcasted_iota(I32, (PAIR, PAIR), 1)
    same_blk = (rowp >= HEAD_DIM) == (colp >= HEAD_DIM)
    strict_lo = (colp % HEAD_DIM) < (rowp % HEAD_DIM)
    m_abd = same_blk & strict_lo
    m_ank = jnp.logical_not(same_blk) & strict_lo
    rowc = lax.broadcasted_iota(I32, (CHUNK, PAIR), 0)
    colc = lax.broadcasted_iota(I32, (CHUNK, PAIR), 1)
    incl_lo2 = (lax.broadcasted_iota(I32, (CHUNK, 2 * PAIR), 1) % HEAD_DIM) <= \
        lax.broadcasted_iota(I32, (CHUNK, 2 * PAIR), 0)
    head0_c = colc < HEAD_DIM
    head0_p = colp < HEAD_DIM

    pairs = range(RWKV_HEADS // 2)
    lanes = [slice(g * PAIR, (g + 1) * PAIR) for g in pairs]
    eye = (rowp == colp).astype(F32)

    def phase_a(chunks):
        chains = [(c, g) for c in chunks for g in pairs]
        prepped = {c: prep(c, pl.ds(c * CHUNK, CHUNK)) for c in chunks}
        gtop = []
        for c, g in chains:
            nb, rb, ub, kb, _ = prepped[c]
            sl = lanes[g]
            nr = jnp.concatenate([nb[:, sl], rb[:, sl]], axis=0)
            uu = ub[:, sl]
            kc = kb[:, sl]
            zero = jnp.zeros_like(nr)
            rhs = jnp.concatenate(
                [jnp.where(head0_p, jnp.concatenate([uu, kc], axis=0), zero),
                 jnp.where(head0_p, zero, jnp.concatenate([kc, uu], axis=0))], axis=0)
            gg = _dot_nt(nr, rhs)
            ar_scr[c, g] = jnp.where(incl_lo2, gg[CHUNK:PAIR], 0.0).astype(BF16)
            gtop.append(jnp.concatenate([gg[0:CHUNK, 0:PAIR], gg[0:CHUNK, PAIR:]], axis=0))
        for i, (c, g) in enumerate(chains):
            vv = prepped[c][4][:, lanes[g]]
            ank = jnp.where(m_ank, gtop[i], 0.0).astype(BF16)
            ankv_scr[c, g] = _dot(ank, jnp.concatenate([vv, vv], axis=0))
        n = range(len(chains))
        pw = [jnp.where(m_abd, gtop[i], 0.0) for i in n]
        tm = [eye + pw[i] for i in n]
        pwb = [x.astype(BF16) for x in pw]
        pw = [_dot(x, x) for x in pwb]
        for _ in range(4):
            pwb = [x.astype(BF16) for x in pw]
            both = [_dot(pwb[i], jnp.concatenate([pwb[i], tm[i].astype(BF16)], axis=1))
                    for i in n]
            pw = [x[:, 0:PAIR] for x in both]
            tm = [tm[i] + both[i][:, PAIR:] for i in n]
        tm = [tm[i] + _dot(pw[i].astype(BF16), tm[i].astype(BF16)) for i in n]
        for i, (c, g) in enumerate(chains):
            t_scr[c, g] = tm[i].astype(BF16)

    nchunks = tb // CHUNK
    for c0 in range(0, nchunks, A_GROUP):
        phase_a(list(range(c0, min(c0 + A_GROUP, nchunks))))

    def phase_b(c):
        rows = pl.ds(c * CHUNK, CHUNK)
        wc_all = wc_scr[c][SUBLANES - 1:SUBLANES, :]
        s_old = [s_scr[g] for g in pairs]
        nrh = [_dot_nt(jnp.concatenate([n_scr[rows, lanes[g]], r_scr[rows, lanes[g]]], axis=0),
                       s_old[g].astype(BF16)) for g in pairs]
        zf = []
        for g in pairs:
            nh = nrh[g][0:CHUNK]
            bf = jnp.concatenate([nh, nh], axis=0) + ankv_scr[c, g]
            zf.append(_dot(t_scr[c, g], bf.astype(BF16)))
        ys = []
        for g in pairs:
            sl = lanes[g]
            vv = v_scr[rows, sl]
            zb = jnp.where(head0_c, zf[g][0:CHUNK], zf[g][CHUNK:PAIR]).astype(BF16)
            zv = jnp.concatenate([zb, vv], axis=0)
            vz = jnp.concatenate([vv, zb], axis=0)
            zero = jnp.zeros_like(zv)
            rhs = jnp.concatenate([jnp.where(head0_p, zv, zero), jnp.where(head0_p, zero, vz)],
                                  axis=0)
            ys.append(nrh[g][CHUNK:PAIR] + _dot(ar_scr[c, g], rhs))
            uk = jnp.concatenate([u_scr[rows, sl], k_scr[rows, sl]], axis=0)
            s_new = (s_old[g] + _dot_tn(zv, uk)) * wc_all[:, sl]
            s_scr[g] = jnp.where(same_blk, s_new, 0.0)
        y = jnp.concatenate(ys, axis=1)
        inv_d = np.float32(1.0 / HEAD_DIM)
        yc = y - head_sum(y) * inv_d
        var = head_sum(yc * yc) * inv_d
        y = yc * lax.rsqrt(var + GN_EPS) * gng_ref[...] + gnb_ref[...] + bv_scr[rows, :]
        gt = gate_ref[rows, :]
        o_ref[rows, :] = (y * (gt * _sigmoid(gt))).astype(o_ref.dtype)

    for c in range(nchunks):
        phase_b(c)
    carry_rkv[0:1, :] = rkv_ref[tb - 1:tb, :]
    carry_sm[0:1, :] = sm_ref[tb - 1:tb, :]


def _rwkv(p, b, t, tb, consts):
    nt = t // tb
    W = RWKV_WIDTH
    row_spec = lambda width, cb: pl.BlockSpec((tb, width), lambda bi, ti: (bi * nt + ti, cb))
    full = lambda arr: pl.BlockSpec(arr.shape, lambda bi, ti: (0,) * arr.ndim)
    in_specs = [row_spec(3 * W, C_RKV // (3 * W)), row_spec(LANES, C_SM // LANES),
                row_spec(W, C_GR // W)] + [full(c) for c in consts]
    return pl.pallas_call(
        functools.partial(_rwkv_kernel, tb=tb),
        grid=(b, nt),
        in_specs=in_specs,
        out_specs=pl.BlockSpec((tb, W), lambda bi, ti: (bi * nt + ti, 0)),
        out_shape=jax.ShapeDtypeStruct((b * t, W), BF16),
        scratch_shapes=[pltpu.VMEM((SUBLANES, 3 * W), F32), pltpu.VMEM((SUBLANES, LANES), F32),
                        pltpu.VMEM((RWKV_HEADS // 2, PAIR, PAIR), F32),
                        pltpu.VMEM((tb, W), BF16), pltpu.VMEM((tb, W), BF16),
                        pltpu.VMEM((tb, W), BF16), pltpu.VMEM((tb, W), BF16),
                        pltpu.VMEM((tb, W), BF16),
                        pltpu.VMEM((tb // CHUNK, SUBLANES, W), F32),
                        pltpu.VMEM((tb, W), F32),
                        pltpu.VMEM((tb // CHUNK, RWKV_HEADS // 2, PAIR, PAIR), BF16),
                        pltpu.VMEM((tb // CHUNK, RWKV_HEADS // 2, PAIR, PAIR), F32),
                        pltpu.VMEM((tb // CHUNK, RWKV_HEADS // 2, CHUNK, 2 * PAIR), BF16)],
        compiler_params=_params(("parallel", "arbitrary")),
        name="rwkv",
    )(p, p, p, *consts)


def _dsaprep_kernel(qd_ref, kv_ref, sm_ref, wi_ref, qg_ref, kvg_ref, lng_ref, lnb_ref,
                    wqidx_ref, wabs_ref, wuv_ref,
                    ckv_ref, kidx_ref, vt_ref, qidx_ref, qabs_ref, wit_ref, *, tb):
    nq = tb // QB
    qd = qd_ref[...]
    cq = qd * lax.rsqrt(jnp.mean(qd * qd, axis=-1, keepdims=True) + NORM_EPS) * qg_ref[...]
    cqb = cq.astype(BF16)
    kv = kv_ref[...]
    ckv = kv * lax.rsqrt(jnp.mean(kv * kv, axis=-1, keepdims=True) + NORM_EPS) * kvg_ref[...]
    ckvb = ckv.astype(BF16)

    sm = sm_ref[...]
    lane = lax.broadcasted_iota(I32, sm.shape, 1)
    hi_half = lane >= IDX_DIM
    inv_d = np.float32(1.0 / IDX_DIM)
    mu = jnp.sum(jnp.where(hi_half, sm, 0.0), axis=-1, keepdims=True) * inv_d
    xc = jnp.where(hi_half, sm - mu, 0.0)
    var = jnp.sum(xc * xc, axis=-1, keepdims=True) * inv_d
    kidx = (xc * lax.rsqrt(var + LN_EPS) * lng_ref[...] + lnb_ref[...]).astype(BF16)

    vt = _dot_nt(wuv_ref[...], ckvb).astype(BF16)
    ones = jnp.ones((VROWS - HEAD_DIM, tb), BF16)
    vt = jnp.concatenate(
        [x for h in range(ATTN_HEADS) for x in (vt[h * HEAD_DIM:(h + 1) * HEAD_DIM], ones)],
        axis=0)
    wit = jnp.transpose(wi_ref[...])[0:SUBLANES, :] * np.float32(IDX_HEADS ** -0.5)
    for j in range(nq):
        qs = slice(j * QB, (j + 1) * QB)
        ckv_ref[0, j] = ckvb[qs, :]
        kidx_ref[0, j] = kidx[qs, :]
        vt_ref[0, j] = vt[:, qs]
    wit_ref[0] = wit
    for h in range(IDX_HEADS):
        res = _dot_nt(wqidx_ref[h], cqb).astype(BF16)
        for j in range(nq):
            qidx_ref[0, :, (j * IDX_HEADS + h) * QB:(j * IDX_HEADS + h + 1) * QB] = \
                res[:, j * QB:(j + 1) * QB]
    for h in range(ATTN_HEADS):
        res = _dot_nt(wabs_ref[h].astype(BF16), cqb).astype(BF16)
        for j in range(nq):
            qabs_ref[0, :, (j * ATTN_HEADS + h) * QB:(j * ATTN_HEADS + h + 1) * QB] = \
                res[:, j * QB:(j + 1) * QB]


def _dsaprep(p, b, t, tb, consts):
    nt = t // tb
    nb = t // QB
    nq = tb // QB
    row_spec = lambda width, cb: pl.BlockSpec((tb, width), lambda bi, ti: (bi * nt + ti, cb))
    full = lambda arr: pl.BlockSpec(arr.shape, lambda bi, ti: (0,) * arr.ndim)
    in_specs = [row_spec(Q_RANK, C_QD // Q_RANK), row_spec(KV_RANK, C_KV // KV_RANK),
                row_spec(LANES, C_SM // LANES), row_spec(LANES, C_WI // LANES)] + \
               [full(c) for c in consts]
    out_shape = [jax.ShapeDtypeStruct((b, nb, QB, KV_RANK), BF16),
                 jax.ShapeDtypeStruct((b, nb, QB, LANES), BF16),
                 jax.ShapeDtypeStruct((b, nb, ATTN_HEADS * VROWS, QB), BF16),
                 jax.ShapeDtypeStruct((b, LANES, nb * IDX_HEADS * QB), BF16),
                 jax.ShapeDtypeStruct((b, KV_RANK, nb * ATTN_HEADS * QB), BF16),
                 jax.ShapeDtypeStruct((b, SUBLANES, t), F32)]
    out_specs = [pl.BlockSpec((1, nq, QB, KV_RANK), lambda bi, ti: (bi, ti, 0, 0)),
                 pl.BlockSpec((1, nq, QB, LANES), lambda bi, ti: (bi, ti, 0, 0)),
                 pl.BlockSpec((1, nq, ATTN_HEADS * VROWS, QB), lambda bi, ti: (bi, ti, 0, 0)),
                 pl.BlockSpec((1, LANES, nq * IDX_HEADS * QB), lambda bi, ti: (bi, 0, ti)),
                 pl.BlockSpec((1, KV_RANK, nq * ATTN_HEADS * QB), lambda bi, ti: (bi, 0, ti)),
                 pl.BlockSpec((1, SUBLANES, tb), lambda bi, ti: (bi, 0, ti))]
    return pl.pallas_call(
        functools.partial(_dsaprep_kernel, tb=tb),
        grid=(b, nt),
        in_specs=in_specs,
        out_specs=out_specs,
        out_shape=out_shape,
        compiler_params=_params(("parallel", "parallel")),
        name="dsaprep",
    )(p, p, p, p, *consts)


INT_MIN = np.int32(-2 ** 31)
KEY_NEG_INF = np.int32(np.array(0xFF800000, np.uint32).view(np.int32) ^ np.int32(0x7FFFFFFF))


def _dsa_kernel(kidx_ref, ckv_ref, vt_ref, qidx_ref, qabs_ref, wi_ref, gate_ref, tri_ref,
                o_ref, sc_scr, acc_scr, *, topk):
    i = pl.program_id(1)
    nch = i + 1
    row = lax.broadcasted_iota(I32, (QB, QB), 0)
    col = lax.broadcasted_iota(I32, (QB, QB), 1)
    qpos = i * QB + col
    wi = wi_ref[0]

    def score_body(c, carry):
        kc = kidx_ref[0, c]
        sc = jnp.zeros((QB, QB), F32)
        for h in range(IDX_HEADS):
            lg = _dot(kc, qidx_ref[0, :, h * QB:(h + 1) * QB])
            sc = sc + wi[h:h + 1, :] * jnp.maximum(lg, 0.0)
        sc_scr[c] = jnp.where(c * QB + row <= qpos, sc, -jnp.inf)
        return carry

    lax.fori_loop(0, nch, score_body, 0)

    sc_scr[nch] = jnp.full((QB, QB), -jnp.inf, F32)

    def count(pred):
        def body(j, accs):
            a0, a1 = accs
            a0 = a0 + _fold8(jnp.where(pred(sc_scr[2 * j]), 1, 0), jnp.add)
            a1 = a1 + _fold8(jnp.where(pred(sc_scr[2 * j + 1]), 1, 0), jnp.add)
            return a0, a1
        zero = jnp.zeros((SUBLANES, QB), I32)
        a0, a1 = lax.fori_loop(0, (nch + 1) // 2, body, (zero, zero))
        return jnp.sum(a0 + a1, axis=0, keepdims=True)

    def key_value(key):
        return lax.bitcast_convert_type(jnp.where(key < 0, key ^ np.int32(0x7FFFFFFF), key), F32)

    thr_key = jnp.full((1, QB), INT_MIN, I32)
    for bit in range(31, -1, -1):
        cand = jnp.zeros((1, QB), I32) if bit == 31 else thr_key | np.int32(1 << bit)
        cand_val = key_value(cand)
        cnt = count(lambda s, v=cand_val: s >= v)
        thr_key = jnp.where(cnt >= topk, cand, thr_key)
    thr = jnp.where(thr_key <= KEY_NEG_INF, -jnp.inf, key_value(thr_key))
    need = (topk - count(lambda s: s > thr)).astype(F32)

    tri = tri_ref[...]

    acc_scr[...] = jnp.zeros_like(acc_scr)

    def attend(c, carry):
        run, ms = carry
        key = sc_scr[c]
        eq = key == thr
        eqf = jnp.where(eq, 1.0, 0.0)
        before = _dot(tri, eqf.astype(BF16)) + run
        sel = (key > thr) | (eq & (before < need))
        sel = sel & (c * QB + row <= qpos)
        bias = jnp.where(sel, 0.0, -jnp.inf)
        run = run + _colsum(eqf)
        heads = range(ATTN_HEADS)
        s = _dot(ckv_ref[0, c], qabs_ref[0])
        sh = [s[:, h * QB:(h + 1) * QB] + bias for h in heads]
        new_ms = [jnp.maximum(ms[h], _colmax(sh[h])) for h in heads]
        shift = [jnp.where(m == -jnp.inf, 0.0, m) for m in new_ms]
        ps = [jnp.exp2((sh[h] - shift[h]).astype(BF16)) for h in heads]
        outs = [_dot(vt_ref[0, c, h * VROWS:(h + 1) * VROWS, :], ps[h]) for h in heads]
        scales = [jnp.broadcast_to(jnp.exp2(ms[h] - shift[h]), (VROWS, QB)) for h in heads]
        acc_scr[...] = acc_scr[...] * jnp.concatenate(scales, axis=0) + \
            jnp.concatenate(outs, axis=0)
        return run, tuple(new_ms)

    ninf = jnp.full((1, QB), -jnp.inf, F32)
    lax.fori_loop(0, nch, attend,
                  (jnp.zeros((1, QB), F32), tuple(ninf for _ in range(ATTN_HEADS))))

    o = jnp.concatenate(
        [acc_scr[h * VROWS:h * VROWS + HEAD_DIM, :] *
         (1.0 / acc_scr[h * VROWS + HEAD_DIM:h * VROWS + HEAD_DIM + 1, :])
         for h in range(ATTN_HEADS)], axis=0)
    o = jnp.transpose(o)
    gt = gate_ref[...]
    o_ref[...] = (o * (gt * _sigmoid(gt))).astype(o_ref.dtype)


def _dsa(p, prep, tri, b, t, topk):
    nb = t // QB
    ckv, kidx, vt, qidx, qabs, wit = prep
    in_specs = [pl.BlockSpec((1, nb, QB, LANES), lambda bi, qi: (bi, 0, 0, 0)),
                pl.BlockSpec((1, nb, QB, KV_RANK), lambda bi, qi: (bi, 0, 0, 0)),
                pl.BlockSpec((1, nb, ATTN_HEADS * VROWS, QB), lambda bi, qi: (bi, 0, 0, 0)),
                pl.BlockSpec((1, LANES, IDX_HEADS * QB), lambda bi, qi: (bi, 0, qi)),
                pl.BlockSpec((1, KV_RANK, ATTN_HEADS * QB), lambda bi, qi: (bi, 0, qi)),
                pl.BlockSpec((1, SUBLANES, QB), lambda bi, qi: (bi, 0, qi)),
                pl.BlockSpec((QB, ATTN_WIDTH), lambda bi, qi: (bi * nb + qi, C_GA // ATTN_WIDTH)),
                pl.BlockSpec((QB, QB), lambda bi, qi: (0, 0))]
    return pl.pallas_call(
        functools.partial(_dsa_kernel, topk=topk),
        grid=(b, nb),
        in_specs=in_specs,
        out_specs=pl.BlockSpec((QB, ATTN_WIDTH), lambda bi, qi: (bi * nb + qi, 0)),
        out_shape=jax.ShapeDtypeStruct((b * t, ATTN_WIDTH), BF16),
        scratch_shapes=[pltpu.VMEM((nb + 1, QB, QB), F32),
                        pltpu.VMEM((ATTN_HEADS * VROWS, QB), F32)],
        compiler_params=_params(("parallel", "arbitrary")),
        name="dsa",
    )(kidx, ckv, vt, qidx, qabs, wit, p, tri)


def _outproj_kernel(yr_ref, ya_ref, x_ref, w_ref, g_ref, o_ref):
    w = w_ref[...]
    y = _dot(yr_ref[...], w[0:RWKV_WIDTH]) + _dot(ya_ref[...], w[RWKV_WIDTH:])
    z = x_ref[...] + y
    ms = jnp.mean(z * z, axis=-1, keepdims=True)
    o_ref[...] = z * lax.rsqrt(ms + NORM_EPS) * g_ref[...]


def _outproj(yr, ya, x2, w_out, final_g, tm):
    n, d = x2.shape
    return pl.pallas_call(
        _outproj_kernel,
        grid=(n // tm,),
        in_specs=[pl.BlockSpec((tm, RWKV_WIDTH), lambda i: (i, 0)),
                  pl.BlockSpec((tm, ATTN_WIDTH), lambda i: (i, 0)),
                  pl.BlockSpec((tm, d), lambda i: (i, 0)),
                  pl.BlockSpec(w_out.shape, lambda i: (0, 0)),
                  pl.BlockSpec((1, d), lambda i: (0, 0))],
        out_specs=pl.BlockSpec((tm, d), lambda i: (i, 0)),
        out_shape=jax.ShapeDtypeStruct((n, d), F32),
        compiler_params=_params(("parallel",)),
        name="outproj",
    )(yr, ya, x2, w_out, final_g)


def _pad_rows(w, start, total):
    return jnp.zeros((total, w.shape[1]), w.dtype).at[start:start + w.shape[0]].set(w)


def _layer(x2, b, t, norm_g, w_in, mu_shift, w0, w_up, a0, a_up, k_k, k_a, r_k, gn_g, gn_b,
           q_norm_g, kv_norm_g, w_uq, w_uk, w_uv, w_qidx, kidx_g, kidx_b, w_out, final_g):
    W = RWKV_WIDTH
    d = x2.shape[1]
    o_r, o_k, o_v = 0, W, 2 * W
    o_wd = 3 * W
    o_ad = o_wd + DECAY_LORA
    o_gr = o_ad + AAA_LORA
    o_qd = o_gr + W
    o_kv = o_qd + Q_RANK
    o_ki = o_kv + KV_RANK
    o_wi = o_ki + IDX_DIM
    o_ga = o_wi + IDX_HEADS
    cols = lambda s, n: w_in[:, s:s + n]
    w_all = jnp.concatenate(
        [cols(o_r, 3 * W), cols(o_gr, W), cols(o_ga, ATTN_WIDTH), cols(o_qd, Q_RANK),
         cols(o_kv, KV_RANK), cols(o_wd, DECAY_LORA), cols(o_ad, AAA_LORA), cols(o_ki, IDX_DIM),
         cols(o_wi, IDX_HEADS), jnp.zeros((d, LANES - IDX_HEADS), w_in.dtype)],
        axis=1).astype(BF16)

    row2 = lambda v: v.reshape(1, -1).astype(F32)
    mu_rkv = row2(mu_shift[0:3 * W])
    mu_sm = row2(jnp.concatenate([mu_shift[3 * W:], jnp.zeros((IDX_DIM,), F32)]))
    wup_pad = _pad_rows(w_up, 0, LANES).astype(BF16)
    aup_pad = _pad_rows(a_up, DECAY_LORA, LANES).astype(BF16)
    hid = np.arange(W // 2) // HEAD_DIM
    bd = jnp.asarray((hid[:, None] == hid[None, :]).astype(np.float32), dtype=BF16)

    tb_r = min(256, t)
    ti = np.arange(CHUNK)
    tri_r = jnp.asarray((ti[None, :] <= ti[:, None]).astype(np.float32), dtype=BF16)
    rwkv_consts = [mu_rkv, mu_sm, row2(w0), wup_pad, row2(a0), aup_pad, row2(k_k), row2(k_a),
                   row2(r_k), row2(gn_g), row2(gn_b), bd, tri_r]

    wq_t = jnp.transpose(w_qidx.reshape(Q_RANK, IDX_HEADS, IDX_DIM), (1, 2, 0))
    wq_t = jnp.concatenate([jnp.zeros_like(wq_t), wq_t], axis=1).astype(BF16)
    w_uq_heads = jnp.transpose(w_uq.reshape(Q_RANK, ATTN_HEADS, HEAD_DIM), (1, 0, 2))
    wabs_t = _absorb(w_uk, w_uq_heads)
    wuv_t = jnp.transpose(w_uv, (0, 2, 1)).reshape(ATTN_WIDTH, KV_RANK).astype(BF16)
    lng = row2(jnp.concatenate([jnp.zeros((IDX_DIM,), F32), kidx_g]))
    lnb = row2(jnp.concatenate([jnp.zeros((IDX_DIM,), F32), kidx_b]))
    prep_consts = [row2(q_norm_g), row2(kv_norm_g), lng, lnb, wq_t, wabs_t, wuv_t]
    ki = np.arange(QB)
    tri_q = jnp.asarray((ki[None, :] < ki[:, None]).astype(np.float32), dtype=BF16)

    tm = min(256, b * t)
    p = _inproj(x2, row2(norm_g), w_all, tm)
    y_r = _rwkv(p, b, t, tb_r, rwkv_consts)
    prep = _dsaprep(p, b, t, min(256, t), prep_consts)
    y_a = _dsa(p, prep, tri_q, b, t, min(TOPK_MAX, t // 4))
    return _outproj(y_r, y_a, x2, w_out.astype(BF16), row2(final_g), tm)


def kernel(x, norm_g, w_in, mu_shift, w0, w_up, a0, a_up, k_k, k_a, r_k, gn_g, gn_b,
           q_norm_g, kv_norm_g, w_uq, w_uk, w_uv, w_qidx, kidx_g, kidx_b, w_out, final_g):
    b, t, d = x.shape
    assert norm_g.shape[0] == 1, "single-layer problem"
    out = _layer(x.reshape(b * t, d), b, t, norm_g[0], w_in[0], mu_shift[0], w0[0], w_up[0],
                 a0[0], a_up[0], k_k[0], k_a[0], r_k[0], gn_g[0], gn_b[0], q_norm_g[0],
                 kv_norm_g[0], w_uq[0], w_uk[0], w_uv[0], w_qidx[0], kidx_g[0], kidx_b[0],
                 w_out[0], final_g)
    return out.reshape(b, t, d)
```

```python
import functools
from typing import NamedTuple

import numpy as np
import jax
import jax.numpy as jnp
from jax import lax
from jax.experimental import pallas as pl
from jax.experimental.pallas import tpu as pltpu

F32 = jnp.float32
BF16 = jnp.bfloat16
I32 = jnp.int32

HEAD_DIM = 64
RWKV_HEADS = 8
RWKV_WIDTH = RWKV_HEADS * HEAD_DIM
DECAY_LORA = 32
AAA_LORA = 32
ATTN_HEADS = 8
ATTN_WIDTH = ATTN_HEADS * HEAD_DIM
Q_RANK = 256
KV_RANK = 128
IDX_HEADS = 4
IDX_DIM = 64
TOPK_MAX = 256
NORM_EPS = 1e-6
LN_EPS = 1e-5
GN_EPS = 64e-5

LANES = 128
SUBLANES = 8
QB = 256
CHUNK = 64
PAIR = 2 * HEAD_DIM
A_GROUP = 2
VROWS = HEAD_DIM + 16

C_RKV = 0
C_GR = 3 * RWKV_WIDTH
C_GA = C_GR + RWKV_WIDTH
C_QD = C_GA + ATTN_WIDTH
C_KV = C_QD + Q_RANK
C_SM = C_KV + KV_RANK
C_WI = C_SM + LANES
P_COLS = C_WI + LANES

VMEM_LIMIT = 56 * 1024 * 1024


def _params(sem):
    return pltpu.CompilerParams(dimension_semantics=sem, vmem_limit_bytes=VMEM_LIMIT)


def _dot(a, b):
    return jnp.dot(a, b, preferred_element_type=F32)


def _dot_nt(a, b):
    return lax.dot_general(a, b, (((1,), (1,)), ((), ())), preferred_element_type=F32)


def _dot_tn(a, b):
    return lax.dot_general(a, b, (((0,), (0,)), ((), ())), preferred_element_type=F32)


def _split2(x):
    hi = x.astype(BF16)
    lo = (x - hi.astype(F32)).astype(BF16)
    return hi, lo


def _split3(x):
    hi = x.astype(BF16)
    r1 = x - hi.astype(F32)
    mid = r1.astype(BF16)
    lo = (r1 - mid.astype(F32)).astype(BF16)
    return hi, mid, lo


def _dot_exact_rhs(x, m_bf16, parts=2):
    if parts == 2:
        hi, lo = _split2(x)
        return _dot(hi, m_bf16) + _dot(lo, m_bf16)
    hi, mid, lo = _split3(x)
    return _dot(hi, m_bf16) + _dot(mid, m_bf16) + _dot(lo, m_bf16)


def _fold8(x, op):
    acc = x[0:SUBLANES]
    for j in range(1, x.shape[0] // SUBLANES):
        acc = op(acc, x[j * SUBLANES:(j + 1) * SUBLANES])
    return acc


def _colsum(x):
    return jnp.sum(_fold8(x, jnp.add), axis=0, keepdims=True)


def _colmax(x):
    return jnp.max(_fold8(x, jnp.maximum), axis=0, keepdims=True)


def _sigmoid(x):
    return 1.0 / (1.0 + jnp.exp(-x))


def _absorb_kernel(wuk_ref, wuq_ref, o_ref):
    a = wuk_ref[0]
    b = wuq_ref[0]
    ah, al = _split2(a)
    bh, bl = _split2(b)
    acc = _dot_nt(ah, bh) + _dot_nt(ah, bl) + _dot_nt(al, bh)
    o_ref[0] = acc * np.float32(HEAD_DIM ** -0.5 * np.log2(np.e))


def _absorb(w_uk, w_uq_heads):
    return pl.pallas_call(
        _absorb_kernel,
        grid=(ATTN_HEADS,),
        in_specs=[pl.BlockSpec((1, KV_RANK, HEAD_DIM), lambda h: (h, 0, 0)),
                  pl.BlockSpec((1, Q_RANK, HEAD_DIM), lambda h: (h, 0, 0))],
        out_specs=pl.BlockSpec((1, KV_RANK, Q_RANK), lambda h: (h, 0, 0)),
        out_shape=jax.ShapeDtypeStruct((ATTN_HEADS, KV_RANK, Q_RANK), F32),
        compiler_params=_params(("arbitrary",)),
        name="absorb",
    )(w_uk, w_uq_heads)


def _inproj_kernel(x_ref, g_ref, w_ref, o_ref):
    x = x_ref[...]
    ms = jnp.mean(x * x, axis=-1, keepdims=True)
    xn = x * lax.rsqrt(ms + NORM_EPS) * g_ref[...]
    o_ref[...] = _dot(xn.astype(BF16), w_ref[...])


def _inproj(x2, norm_g, w_all, tm):
    n, d = x2.shape
    return pl.pallas_call(
        _inproj_kernel,
        grid=(n // tm,),
        in_specs=[pl.BlockSpec((tm, d), lambda i: (i, 0)),
                  pl.BlockSpec((1, d), lambda i: (0, 0)),
                  pl.BlockSpec((d, P_COLS), lambda i: (0, 0))],
        out_specs=pl.BlockSpec((tm, P_COLS), lambda i: (i, 0)),
        out_shape=jax.ShapeDtypeStruct((n, P_COLS), F32),
        compiler_params=_params(("parallel",)),
        name="inproj",
    )(x2, norm_g, w_all)


def _rwkv_kernel(rkv_ref, sm_ref, gate_ref, mu_rkv_ref, mu_sm_ref, w0_ref, wup_ref, a0_ref,
                 aup_ref, kkw_ref, ka_ref, rk_ref, gng_ref, gnb_ref, bd_ref, tri_ref,
                 o_ref,
                 carry_rkv, carry_sm, s_scr, n_scr, r_scr, u_scr, k_scr, v_scr, wc_scr, bv_scr,
                 t_scr, ankv_scr, ar_scr, *, tb):
    t = pl.program_id(1)
    W = RWKV_WIDTH

    @pl.when(t == 0)
    def _():
        carry_rkv[...] = jnp.zeros_like(carry_rkv)
        carry_sm[...] = jnp.zeros_like(carry_sm)
        s_scr[...] = jnp.zeros_like(s_scr)

    bd = bd_ref[...]
    half = W // 2

    def head_sum(x):
        st = jnp.concatenate([x[:, :half], x[:, half:]], axis=0)
        res = _dot(st.astype(BF16), bd)
        return jnp.concatenate([res[:CHUNK], res[CHUNK:]], axis=1)

    def shifted(ref, carry, c, rows):
        cur = ref[rows, :]
        first = carry[0:1, :] if c == 0 else ref[c * CHUNK - 1:c * CHUNK, :]
        row = lax.broadcasted_iota(I32, (CHUNK, 1), 0)
        return cur, jnp.where(row == 0, first, pltpu.roll(cur, 1, 0))

    def prep(c):
        rows = pl.ds(c * CHUNK, CHUNK)
        p, prev = shifted(rkv_ref, carry_rkv, c, rows)
        xs = p + (prev - p) * mu_rkv_ref[...]
        r = xs[:, 0:W]
        k = xs[:, W:2 * W]
        v = xs[:, 2 * W:3 * W]
        ps, prev_s = shifted(sm_ref, carry_sm, c, rows)
        xl = ps + (prev_s - ps) * mu_sm_ref[...]
        zw = w0_ref[...] + _dot(jnp.tanh(xl).astype(BF16), wup_ref[...])
        za = a0_ref[...] + _dot(xl.astype(BF16), aup_ref[...])
        yield
        logw = -np.float32(np.exp(-0.5)) * _sigmoid(zw)
        a = _sigmoid(za)
        kk = k * kkw_ref[...]
        ss = head_sum(kk * kk)
        hi, mid, lo = _split3(logw)
        tri = tri_ref[...]
        cum = _dot(tri, hi) + _dot(tri, mid) + _dot(tri, lo)
        yield
        kk = kk * (1.0 / jnp.maximum(jnp.sqrt(ss), 1e-12))
        k2 = k * (1.0 + (a - 1.0) * ka_ref[...])
        bonus = head_sum(r * k2 * rk_ref[...])
        wt = jnp.exp(cum)
        wi = jnp.exp(-cum)
        wx = jnp.exp(cum - logw)
        n_scr[rows, :] = (-kk * wx).astype(BF16)
        r_scr[rows, :] = (r * wt).astype(BF16)
        u_scr[rows, :] = (kk * a * wi).astype(BF16)
        k_scr[rows, :] = (k2 * wi).astype(BF16)
        v_scr[rows, :] = v.astype(BF16)
        wc_scr[c] = wt[CHUNK - SUBLANES:CHUNK, :]
        yield
        bv_scr[rows, :] = bonus * v

    rowp = lax.broadcasted_iota(I32, (PAIR, PAIR), 0)
    colp = lax.broadcasted_iota(I32, (PAIR, PAIR), 1)
    same_blk = (rowp >= HEAD_DIM) == (colp >= HEAD_DIM)
    strict_lo = (colp % HEAD_DIM) < (rowp % HEAD_DIM)
    m_abd = same_blk & strict_lo
    m_ank = jnp.logical_not(same_blk) & strict_lo
    rowc = lax.broadcasted_iota(I32, (CHUNK, PAIR), 0)
    colc = lax.broadcasted_iota(I32, (CHUNK, PAIR), 1)
    incl_lo2 = (lax.broadcasted_iota(I32, (CHUNK, 2 * PAIR), 1) % HEAD_DIM) <= \
        lax.broadcasted_iota(I32, (CHUNK, 2 * PAIR), 0)
    head0_c = colc < HEAD_DIM
    head0_p = colp < HEAD_DIM

    pairs = range(RWKV_HEADS // 2)
    lanes = [slice(g * PAIR, (g + 1) * PAIR) for g in pairs]
    eye = (rowp == colp).astype(F32)

    def phase_a(chunks):
        chains = [(c, g) for c in chunks for g in pairs]
        gtop = []
        for c, g in chains:
            rows = pl.ds(c * CHUNK, CHUNK)
            sl = lanes[g]
            nr = jnp.concatenate([n_scr[rows, sl], r_scr[rows, sl]], axis=0)
            uu = u_scr[rows, sl]
            kc = k_scr[rows, sl]
            zero = jnp.zeros_like(nr)
            rhs = jnp.concatenate(
                [jnp.where(head0_p, jnp.concatenate([uu, kc], axis=0), zero),
                 jnp.where(head0_p, zero, jnp.concatenate([kc, uu], axis=0))], axis=0)
            gg = _dot_nt(nr, rhs)
            ar_scr[c, g] = jnp.where(incl_lo2, gg[CHUNK:PAIR], 0.0).astype(BF16)
            gtop.append(jnp.concatenate([gg[0:CHUNK, 0:PAIR], gg[0:CHUNK, PAIR:]], axis=0))
        yield
        for i, (c, g) in enumerate(chains):
            vv = v_scr[pl.ds(c * CHUNK, CHUNK), lanes[g]]
            ank = jnp.where(m_ank, gtop[i], 0.0).astype(BF16)
            ankv_scr[c, g] = _dot(ank, jnp.concatenate([vv, vv], axis=0))
        n = range(len(chains))
        pw = [jnp.where(m_abd, gtop[i], 0.0) for i in n]
        tm = [eye + pw[i] for i in n]
        pwb = [x.astype(BF16) for x in pw]
        pw = [_dot(x, x) for x in pwb]
        yield
        for _ in range(4):
            pwb = [x.astype(BF16) for x in pw]
            both = [_dot(pwb[i], jnp.concatenate([pwb[i], tm[i].astype(BF16)], axis=1))
                    for i in n]
            pw = [x[:, 0:PAIR] for x in both]
            tm = [tm[i] + both[i][:, PAIR:] for i in n]
            yield
        tm = [tm[i] + _dot(pw[i].astype(BF16), tm[i].astype(BF16)) for i in n]
        for i, (c, g) in enumerate(chains):
            t_scr[c, g] = tm[i].astype(BF16)

    def phase_b(c):
        rows = pl.ds(c * CHUNK, CHUNK)
        wc_all = wc_scr[c][SUBLANES - 1:SUBLANES, :]
        s_old = [s_scr[g] for g in pairs]
        nrh = [_dot_nt(jnp.concatenate([n_scr[rows, lanes[g]], r_scr[rows, lanes[g]]], axis=0),
                       s_old[g].astype(BF16)) for g in pairs]
        yield
        zf = []
        for g in pairs:
            nh = nrh[g][0:CHUNK]
            bf = jnp.concatenate([nh, nh], axis=0) + ankv_scr[c, g]
            zf.append(_dot(t_scr[c, g], bf.astype(BF16)))
        yield
        ys = []
        for g in pairs:
            sl = lanes[g]
            vv = v_scr[rows, sl]
            zb = jnp.where(head0_c, zf[g][0:CHUNK], zf[g][CHUNK:PAIR]).astype(BF16)
            zv = jnp.concatenate([zb, vv], axis=0)
            vz = jnp.concatenate([vv, zb], axis=0)
            zero = jnp.zeros_like(zv)
            rhs = jnp.concatenate([jnp.where(head0_p, zv, zero), jnp.where(head0_p, zero, vz)],
                                  axis=0)
            ys.append(nrh[g][CHUNK:PAIR] + _dot(ar_scr[c, g], rhs))
            uk = jnp.concatenate([u_scr[rows, sl], k_scr[rows, sl]], axis=0)
            s_new = (s_old[g] + _dot_tn(zv, uk)) * wc_all[:, sl]
            s_scr[g] = jnp.where(same_blk, s_new, 0.0)
        yield
        y = jnp.concatenate(ys, axis=1)
        inv_d = np.float32(1.0 / HEAD_DIM)
        yc = y - head_sum(y) * inv_d
        yield
        var = head_sum(yc * yc) * inv_d
        y = yc * lax.rsqrt(var + GN_EPS) * gng_ref[...] + gnb_ref[...] + bv_scr[rows, :]
        gt = gate_ref[rows, :]
        o_ref[rows, :] = (y * (gt * _sigmoid(gt))).astype(o_ref.dtype)

    def in_step(gens):
        gens = list(gens)
        while gens:
            gens = [g for g in gens if next(g, True) is None]
            if gens:
                yield

    def in_turn(gens):
        for g in gens:
            yield from g

    nchunks = tb // CHUNK
    groups = [list(range(c0, c0 + A_GROUP)) for c0 in range(0, nchunks, A_GROUP)]
    for s in range(len(groups) + 2):
        streams = []
        if s - 2 >= 0:
            streams.append(in_turn(phase_b(c) for c in groups[s - 2]))
        if 0 <= s - 1 < len(groups):
            streams.append(phase_a(groups[s - 1]))
        if s < len(groups):
            streams.append(in_step(prep(c) for c in groups[s]))
        while streams:
            streams = [g for g in streams if next(g, True) is None]
    carry_rkv[0:1, :] = rkv_ref[tb - 1:tb, :]
    carry_sm[0:1, :] = sm_ref[tb - 1:tb, :]


def _rwkv(p, b, t, tb, consts):
    nt = t // tb
    W = RWKV_WIDTH
    row_spec = lambda width, cb: pl.BlockSpec((tb, width), lambda bi, ti: (bi * nt + ti, cb))
    full = lambda arr: pl.BlockSpec(arr.shape, lambda bi, ti: (0,) * arr.ndim)
    in_specs = [row_spec(3 * W, C_RKV // (3 * W)), row_spec(LANES, C_SM // LANES),
                row_spec(W, C_GR // W)] + [full(c) for c in consts]
    return pl.pallas_call(
        functools.partial(_rwkv_kernel, tb=tb),
        grid=(b, nt),
        in_specs=in_specs,
        out_specs=pl.BlockSpec((tb, W), lambda bi, ti: (bi * nt + ti, 0)),
        out_shape=jax.ShapeDtypeStruct((b * t, W), BF16),
        scratch_shapes=[pltpu.VMEM((SUBLANES, 3 * W), F32), pltpu.VMEM((SUBLANES, LANES), F32),
                        pltpu.VMEM((RWKV_HEADS // 2, PAIR, PAIR), F32),
                        pltpu.VMEM((tb, W), BF16), pltpu.VMEM((tb, W), BF16),
                        pltpu.VMEM((tb, W), BF16), pltpu.VMEM((tb, W), BF16),
                        pltpu.VMEM((tb, W), BF16),
                        pltpu.VMEM((tb // CHUNK, SUBLANES, W), F32),
                        pltpu.VMEM((tb, W), F32),
                        pltpu.VMEM((tb // CHUNK, RWKV_HEADS // 2, PAIR, PAIR), BF16),
                        pltpu.VMEM((tb // CHUNK, RWKV_HEADS // 2, PAIR, PAIR), F32),
                        pltpu.VMEM((tb // CHUNK, RWKV_HEADS // 2, CHUNK, 2 * PAIR), BF16)],
        compiler_params=_params(("parallel", "arbitrary")),
        name="rwkv",
    )(p, p, p, *consts)


def _dsaprep_kernel(qd_ref, kv_ref, sm_ref, wi_ref, qg_ref, kvg_ref, lng_ref, lnb_ref,
                    wqidx_ref, wabs_ref, wuv_ref,
                    ckv_ref, kidx_ref, vt_ref, qidx_ref, qabs_ref, wit_ref, *, tb):
    nq = tb // QB
    qd = qd_ref[...]
    cq = qd * lax.rsqrt(jnp.mean(qd * qd, axis=-1, keepdims=True) + NORM_EPS) * qg_ref[...]
    cqb = cq.astype(BF16)
    kv = kv_ref[...]
    ckv = kv * lax.rsqrt(jnp.mean(kv * kv, axis=-1, keepdims=True) + NORM_EPS) * kvg_ref[...]
    ckvb = ckv.astype(BF16)

    sm = sm_ref[...]
    lane = lax.broadcasted_iota(I32, sm.shape, 1)
    hi_half = lane >= IDX_DIM
    inv_d = np.float32(1.0 / IDX_DIM)
    mu = jnp.sum(jnp.where(hi_half, sm, 0.0), axis=-1, keepdims=True) * inv_d
    xc = jnp.where(hi_half, sm - mu, 0.0)
    var = jnp.sum(xc * xc, axis=-1, keepdims=True) * inv_d
    kidx = (xc * lax.rsqrt(var + LN_EPS) * lng_ref[...] + lnb_ref[...]).astype(BF16)

    vt = _dot_nt(wuv_ref[...], ckvb).astype(BF16)
    ones = jnp.ones((VROWS - HEAD_DIM, tb), BF16)
    vt = jnp.concatenate(
        [x for h in range(ATTN_HEADS) for x in (vt[h * HEAD_DIM:(h + 1) * HEAD_DIM], ones)],
        axis=0)
    wit = jnp.transpose(wi_ref[...])[0:SUBLANES, :] * np.float32(IDX_HEADS ** -0.5)
    for j in range(nq):
        qs = slice(j * QB, (j + 1) * QB)
        ckv_ref[0, j] = ckvb[qs, :]
        kidx_ref[0, j] = kidx[qs, :]
        vt_ref[0, j] = vt[:, qs]
    wit_ref[0] = wit
    for h in range(IDX_HEADS):
        res = _dot_nt(wqidx_ref[h], cqb).astype(BF16)
        for j in range(nq):
            qidx_ref[0, :, (j * IDX_HEADS + h) * QB:(j * IDX_HEADS + h + 1) * QB] = \
                res[:, j * QB:(j + 1) * QB]
    for h in range(ATTN_HEADS):
        res = _dot_nt(wabs_ref[h].astype(BF16), cqb).astype(BF16)
        for j in range(nq):
            qabs_ref[0, :, (j * ATTN_HEADS + h) * QB:(j * ATTN_HEADS + h + 1) * QB] = \
                res[:, j * QB:(j + 1) * QB]


def _dsaprep(p, b, t, tb, consts):
    nt = t // tb
    nb = t // QB
    nq = tb // QB
    row_spec = lambda width, cb: pl.BlockSpec((tb, width), lambda bi, ti: (bi * nt + ti, cb))
    full = lambda arr: pl.BlockSpec(arr.shape, lambda bi, ti: (0,) * arr.ndim)
    in_specs = [row_spec(Q_RANK, C_QD // Q_RANK), row_spec(KV_RANK, C_KV // KV_RANK),
                row_spec(LANES, C_SM // LANES), row_spec(LANES, C_WI // LANES)] + \
               [full(c) for c in consts]
    out_shape = [jax.ShapeDtypeStruct((b, nb, QB, KV_RANK), BF16),
                 jax.ShapeDtypeStruct((b, nb, QB, LANES), BF16),
                 jax.ShapeDtypeStruct((b, nb, ATTN_HEADS * VROWS, QB), BF16),
                 jax.ShapeDtypeStruct((b, LANES, nb * IDX_HEADS * QB), BF16),
                 jax.ShapeDtypeStruct((b, KV_RANK, nb * ATTN_HEADS * QB), BF16),
                 jax.ShapeDtypeStruct((b, SUBLANES, t), F32)]
    out_specs = [pl.BlockSpec((1, nq, QB, KV_RANK), lambda bi, ti: (bi, ti, 0, 0)),
                 pl.BlockSpec((1, nq, QB, LANES), lambda bi, ti: (bi, ti, 0, 0)),
                 pl.BlockSpec((1, nq, ATTN_HEADS * VROWS, QB), lambda bi, ti: (bi, ti, 0, 0)),
                 pl.BlockSpec((1, LANES, nq * IDX_HEADS * QB), lambda bi, ti: (bi, 0, ti)),
                 pl.BlockSpec((1, KV_RANK, nq * ATTN_HEADS * QB), lambda bi, ti: (bi, 0, ti)),
                 pl.BlockSpec((1, SUBLANES, tb), lambda bi, ti: (bi, 0, ti))]
    return pl.pallas_call(
        functools.partial(_dsaprep_kernel, tb=tb),
        grid=(b, nt),
        in_specs=in_specs,
        out_specs=out_specs,
        out_shape=out_shape,
        compiler_params=_params(("parallel", "parallel")),
        name="dsaprep",
    )(p, p, p, p, *consts)


INT_MIN = np.int32(-2 ** 31)
KEY_NEG_INF = np.int32(np.array(0xFF800000, np.uint32).view(np.int32) ^ np.int32(0x7FFFFFFF))


def _dsa_kernel(kidx_ref, ckv_ref, vt_ref, qidx_ref, qabs_ref, wi_ref, gate_ref, tri_ref,
                o_ref, sc_scr, acc_scr, *, topk):
    i = pl.program_id(1)
    nch = i + 1
    row = lax.broadcasted_iota(I32, (QB, QB), 0)
    col = lax.broadcasted_iota(I32, (QB, QB), 1)
    qpos = i * QB + col
    wi = wi_ref[0]

    nb = kidx_ref.shape[1]
    npairs = (nch + 1) // 2

    def score_body(j, carry):
        for c in (2 * j, 2 * j + 1):
            kc = kidx_ref[0, jnp.minimum(c, nb - 1)]
            lg = _dot(kc, qidx_ref[0])
            sc = jnp.zeros((QB, QB), F32)
            for h in range(IDX_HEADS):
                sc = sc + wi[h:h + 1, :] * jnp.maximum(lg[:, h * QB:(h + 1) * QB], 0.0)
            sc_scr[c] = jnp.where(c * QB + row <= qpos, sc, -jnp.inf)
        return carry

    lax.fori_loop(0, npairs, score_body, 0)

    def count(pred):
        def body(j, accs):
            a0, a1 = accs
            a0 = a0 + _fold8(jnp.where(pred(sc_scr[2 * j]), 1, 0), jnp.add)
            a1 = a1 + _fold8(jnp.where(pred(sc_scr[2 * j + 1]), 1, 0), jnp.add)
            return a0, a1
        zero = jnp.zeros((SUBLANES, QB), I32)
        a0, a1 = lax.fori_loop(0, npairs, body, (zero, zero))
        return jnp.sum(a0 + a1, axis=0, keepdims=True)

    def key_value(key):
        return lax.bitcast_convert_type(jnp.where(key < 0, key ^ np.int32(0x7FFFFFFF), key), F32)

    thr_key = jnp.full((1, QB), INT_MIN, I32)
    for bit in range(31, -1, -1):
        cand = jnp.zeros((1, QB), I32) if bit == 31 else thr_key | np.int32(1 << bit)
        cand_val = key_value(cand)
        cnt = count(lambda s, v=cand_val: s >= v)
        thr_key = jnp.where(cnt >= topk, cand, thr_key)
    thr = jnp.where(thr_key <= KEY_NEG_INF, -jnp.inf, key_value(thr_key))
    need = (topk - count(lambda s: s > thr)).astype(F32)

    tri = tri_ref[...]

    acc_scr[...] = jnp.zeros_like(acc_scr)

    def attend(c, carry):
        run, ms = carry
        key = sc_scr[c]
        eq = key == thr
        eqf = jnp.where(eq, 1.0, 0.0)
        before = _dot(tri, eqf.astype(BF16)) + run
        sel = (key > thr) | (eq & (before < need))
        sel = sel & (c * QB + row <= qpos)
        bias = jnp.where(sel, 0.0, -jnp.inf)
        run = run + _colsum(eqf)
        heads = range(ATTN_HEADS)
        s = _dot(ckv_ref[0, c], qabs_ref[0])
        sh = [s[:, h * QB:(h + 1) * QB] + bias for h in heads]
        new_ms = [jnp.maximum(ms[h], _colmax(sh[h])) for h in heads]
        shift = [jnp.where(m == -jnp.inf, 0.0, m) for m in new_ms]
        ps = [jnp.exp2((sh[h] - shift[h]).astype(BF16)) for h in heads]
        outs = [_dot(vt_ref[0, c, h * VROWS:(h + 1) * VROWS, :], ps[h]) for h in heads]
        scales = [jnp.broadcast_to(jnp.exp2(ms[h] - shift[h]), (VROWS, QB)) for h in heads]
        acc_scr[...] = acc_scr[...] * jnp.concatenate(scales, axis=0) + \
            jnp.concatenate(outs, axis=0)
        return run, tuple(new_ms)

    ninf = jnp.full((1, QB), -jnp.inf, F32)
    lax.fori_loop(0, nch, attend,
                  (jnp.zeros((1, QB), F32), tuple(ninf for _ in range(ATTN_HEADS))))

    o = jnp.concatenate(
        [acc_scr[h * VROWS:h * VROWS + HEAD_DIM, :] *
         (1.0 / acc_scr[h * VROWS + HEAD_DIM:h * VROWS + HEAD_DIM + 1, :])
         for h in range(ATTN_HEADS)], axis=0)
    o = jnp.transpose(o)
    gt = gate_ref[...]
    o_ref[...] = (o * (gt * _sigmoid(gt))).astype(o_ref.dtype)


def _dsa(p, prep, tri, b, t, topk):
    nb = t // QB
    ckv, kidx, vt, qidx, qabs, wit = prep
    in_specs = [pl.BlockSpec((1, nb, QB, LANES), lambda bi, qi: (bi, 0, 0, 0)),
                pl.BlockSpec((1, nb, QB, KV_RANK), lambda bi, qi: (bi, 0, 0, 0)),
                pl.BlockSpec((1, nb, ATTN_HEADS * VROWS, QB), lambda bi, qi: (bi, 0, 0, 0)),
                pl.BlockSpec((1, LANES, IDX_HEADS * QB), lambda bi, qi: (bi, 0, qi)),
                pl.BlockSpec((1, KV_RANK, ATTN_HEADS * QB), lambda bi, qi: (bi, 0, qi)),
                pl.BlockSpec((1, SUBLANES, QB), lambda bi, qi: (bi, 0, qi)),
                pl.BlockSpec((QB, ATTN_WIDTH), lambda bi, qi: (bi * nb + qi, C_GA // ATTN_WIDTH)),
                pl.BlockSpec((QB, QB), lambda bi, qi: (0, 0))]
    return pl.pallas_call(
        functools.partial(_dsa_kernel, topk=topk),
        grid=(b, nb),
        in_specs=in_specs,
        out_specs=pl.BlockSpec((QB, ATTN_WIDTH), lambda bi, qi: (bi * nb + qi, 0)),
        out_shape=jax.ShapeDtypeStruct((b * t, ATTN_WIDTH), BF16),
        scratch_shapes=[pltpu.VMEM((nb + 1, QB, QB), F32),
                        pltpu.VMEM((ATTN_HEADS * VROWS, QB), F32)],
        compiler_params=_params(("parallel", "arbitrary")),
        name="dsa",
    )(kidx, ckv, vt, qidx, qabs, wit, p, tri)


def _outproj_kernel(yr_ref, ya_ref, x_ref, w_ref, g_ref, o_ref):
    w = w_ref[...]
    y = _dot(yr_ref[...], w[0:RWKV_WIDTH]) + _dot(ya_ref[...], w[RWKV_WIDTH:])
    z = x_ref[...] + y
    ms = jnp.mean(z * z, axis=-1, keepdims=True)
    o_ref[...] = z * lax.rsqrt(ms + NORM_EPS) * g_ref[...]


def _outproj(yr, ya, x2, w_out, final_g, tm):
    n, d = x2.shape
    return pl.pallas_call(
        _outproj_kernel,
        grid=(n // tm,),
        in_specs=[pl.BlockSpec((tm, RWKV_WIDTH), lambda i: (i, 0)),
                  pl.BlockSpec((tm, ATTN_WIDTH), lambda i: (i, 0)),
                  pl.BlockSpec((tm, d), lambda i: (i, 0)),
                  pl.BlockSpec(w_out.shape, lambda i: (0, 0)),
                  pl.BlockSpec((1, d), lambda i: (0, 0))],
        out_specs=pl.BlockSpec((tm, d), lambda i: (i, 0)),
        out_shape=jax.ShapeDtypeStruct((n, d), F32),
        compiler_params=_params(("parallel",)),
        name="outproj",
    )(yr, ya, x2, w_out, final_g)


class _BlockRows(NamedTuple):
    inproj_rows: int
    rwkv_rows: int
    outproj_rows: int


def _block_rows(b, t):
    n = b * t
    assert t % QB == 0 and t % (A_GROUP * CHUNK) == 0
    pick = lambda want, total: want if total % want == 0 else QB
    return _BlockRows(inproj_rows=pick(512, n), rwkv_rows=pick(512, t), outproj_rows=pick(1024, n))


def _pad_rows(w, start, total):
    return jnp.zeros((total, w.shape[1]), w.dtype).at[start:start + w.shape[0]].set(w)


def _layer(x2, b, t, norm_g, w_in, mu_shift, w0, w_up, a0, a_up, k_k, k_a, r_k, gn_g, gn_b,
           q_norm_g, kv_norm_g, w_uq, w_uk, w_uv, w_qidx, kidx_g, kidx_b, w_out, final_g):
    W = RWKV_WIDTH
    d = x2.shape[1]
    o_r, o_k, o_v = 0, W, 2 * W
    o_wd = 3 * W
    o_ad = o_wd + DECAY_LORA
    o_gr = o_ad + AAA_LORA
    o_qd = o_gr + W
    o_kv = o_qd + Q_RANK
    o_ki = o_kv + KV_RANK
    o_wi = o_ki + IDX_DIM
    o_ga = o_wi + IDX_HEADS
    cols = lambda s, n: w_in[:, s:s + n]
    w_all = jnp.concatenate(
        [cols(o_r, 3 * W), cols(o_gr, W), cols(o_ga, ATTN_WIDTH), cols(o_qd, Q_RANK),
         cols(o_kv, KV_RANK), cols(o_wd, DECAY_LORA), cols(o_ad, AAA_LORA), cols(o_ki, IDX_DIM),
         cols(o_wi, IDX_HEADS), jnp.zeros((d, LANES - IDX_HEADS), w_in.dtype)],
        axis=1).astype(BF16)

    row2 = lambda v: v.reshape(1, -1).astype(F32)
    mu_rkv = row2(mu_shift[0:3 * W])
    mu_sm = row2(jnp.concatenate([mu_shift[3 * W:], jnp.zeros((IDX_DIM,), F32)]))
    wup_pad = _pad_rows(w_up, 0, LANES).astype(BF16)
    aup_pad = _pad_rows(a_up, DECAY_LORA, LANES).astype(BF16)
    hid = np.arange(W // 2) // HEAD_DIM
    bd = jnp.asarray((hid[:, None] == hid[None, :]).astype(np.float32), dtype=BF16)

    blk = _block_rows(b, t)
    ti = np.arange(CHUNK)
    tri_r = jnp.asarray((ti[None, :] <= ti[:, None]).astype(np.float32), dtype=BF16)
    rwkv_consts = [mu_rkv, mu_sm, row2(w0), wup_pad, row2(a0), aup_pad, row2(k_k), row2(k_a),
                   row2(r_k), row2(gn_g), row2(gn_b), bd, tri_r]

    wq_t = jnp.transpose(w_qidx.reshape(Q_RANK, IDX_HEADS, IDX_DIM), (1, 2, 0))
    wq_t = jnp.concatenate([jnp.zeros_like(wq_t), wq_t], axis=1).astype(BF16)
    w_uq_heads = jnp.transpose(w_uq.reshape(Q_RANK, ATTN_HEADS, HEAD_DIM), (1, 0, 2))
    wabs_t = _absorb(w_uk, w_uq_heads)
    wuv_t = jnp.transpose(w_uv, (0, 2, 1)).reshape(ATTN_WIDTH, KV_RANK).astype(BF16)
    lng = row2(jnp.concatenate([jnp.zeros((IDX_DIM,), F32), kidx_g]))
    lnb = row2(jnp.concatenate([jnp.zeros((IDX_DIM,), F32), kidx_b]))
    prep_consts = [row2(q_norm_g), row2(kv_norm_g), lng, lnb, wq_t, wabs_t, wuv_t]
    ki = np.arange(QB)
    tri_q = jnp.asarray((ki[None, :] < ki[:, None]).astype(np.float32), dtype=BF16)

    p = _inproj(x2, row2(norm_g), w_all, blk.inproj_rows)
    y_r = _rwkv(p, b, t, blk.rwkv_rows, rwkv_consts)
    prep = _dsaprep(p, b, t, QB, prep_consts)
    y_a = _dsa(p, prep, tri_q, b, t, min(TOPK_MAX, t // 4))
    return _outproj(y_r, y_a, x2, w_out.astype(BF16), row2(final_g), blk.outproj_rows)


def kernel(x, norm_g, w_in, mu_shift, w0, w_up, a0, a_up, k_k, k_a, r_k, gn_g, gn_b,
           q_norm_g, kv_norm_g, w_uq, w_uk, w_uv, w_qidx, kidx_g, kidx_b, w_out, final_g):
    b, t, d = x.shape
    assert norm_g.shape[0] == 1, "single-layer problem"
    out = _layer(x.reshape(b * t, d), b, t, norm_g[0], w_in[0], mu_shift[0], w0[0], w_up[0],
                 a0[0], a_up[0], k_k[0], k_a[0], r_k[0], gn_g[0], gn_b[0], q_norm_g[0],
                 kv_norm_g[0], w_uq[0], w_uk[0], w_uv[0], w_qidx[0], kidx_g[0], kidx_b[0],
                 w_out[0], final_g)
    return out.reshape(b, t, d)
```

```python
import functools
from typing import NamedTuple

import numpy as np
import jax
import jax.numpy as jnp
from jax import lax
from jax.experimental import pallas as pl
from jax.experimental.pallas import tpu as pltpu

F32 = jnp.float32
BF16 = jnp.bfloat16
I32 = jnp.int32

HEAD_DIM = 64
RWKV_HEADS = 8
RWKV_WIDTH = RWKV_HEADS * HEAD_DIM
DECAY_LORA = 32
AAA_LORA = 32
ATTN_HEADS = 8
ATTN_WIDTH = ATTN_HEADS * HEAD_DIM
Q_RANK = 256
KV_RANK = 128
IDX_HEADS = 4
IDX_DIM = 64
TOPK_MAX = 256
NORM_EPS = 1e-6
LN_EPS = 1e-5
GN_EPS = 64e-5

LANES = 128
SUBLANES = 8
QB = 256
KB = 2 * QB
CHUNK = 64
PAIR = 2 * HEAD_DIM
A_GROUP = 2
VROWS = HEAD_DIM + 16

C_RKV = 0
C_GR = 3 * RWKV_WIDTH
C_GA = C_GR + RWKV_WIDTH
C_QD = C_GA + ATTN_WIDTH
C_KV = C_QD + Q_RANK
C_SM = C_KV + KV_RANK
C_WI = C_SM + LANES
P_COLS = C_WI + LANES

VMEM_LIMIT = 56 * 1024 * 1024


def _params(sem):
    return pltpu.CompilerParams(dimension_semantics=sem, vmem_limit_bytes=VMEM_LIMIT)


def _dot(a, b):
    return jnp.dot(a, b, preferred_element_type=F32)


def _dot_nt(a, b):
    return lax.dot_general(a, b, (((1,), (1,)), ((), ())), preferred_element_type=F32)


def _dot_tn(a, b):
    return lax.dot_general(a, b, (((0,), (0,)), ((), ())), preferred_element_type=F32)


def _split2(x):
    hi = x.astype(BF16)
    lo = (x - hi.astype(F32)).astype(BF16)
    return hi, lo


def _split3(x):
    hi = x.astype(BF16)
    r1 = x - hi.astype(F32)
    mid = r1.astype(BF16)
    lo = (r1 - mid.astype(F32)).astype(BF16)
    return hi, mid, lo


def _dot_exact_rhs(x, m_bf16, parts=2):
    if parts == 2:
        hi, lo = _split2(x)
        return _dot(hi, m_bf16) + _dot(lo, m_bf16)
    hi, mid, lo = _split3(x)
    return _dot(hi, m_bf16) + _dot(mid, m_bf16) + _dot(lo, m_bf16)


def _fold8(x, op):
    acc = x[0:SUBLANES]
    for j in range(1, x.shape[0] // SUBLANES):
        acc = op(acc, x[j * SUBLANES:(j + 1) * SUBLANES])
    return acc


def _colsum(x):
    return jnp.sum(_fold8(x, jnp.add), axis=0, keepdims=True)


def _colmax(x):
    return jnp.max(_fold8(x, jnp.maximum), axis=0, keepdims=True)


def _sigmoid(x):
    return 1.0 / (1.0 + jnp.exp(-x))


def _absorb_kernel(wuk_ref, wuq_ref, o_ref):
    a = wuk_ref[0]
    b = wuq_ref[0]
    ah, al = _split2(a)
    bh, bl = _split2(b)
    acc = _dot_nt(ah, bh) + _dot_nt(ah, bl) + _dot_nt(al, bh)
    o_ref[0] = acc * np.float32(HEAD_DIM ** -0.5 * np.log2(np.e))


def _absorb(w_uk, w_uq_heads):
    return pl.pallas_call(
        _absorb_kernel,
        grid=(ATTN_HEADS,),
        in_specs=[pl.BlockSpec((1, KV_RANK, HEAD_DIM), lambda h: (h, 0, 0)),
                  pl.BlockSpec((1, Q_RANK, HEAD_DIM), lambda h: (h, 0, 0))],
        out_specs=pl.BlockSpec((1, KV_RANK, Q_RANK), lambda h: (h, 0, 0)),
        out_shape=jax.ShapeDtypeStruct((ATTN_HEADS, KV_RANK, Q_RANK), F32),
        compiler_params=_params(("arbitrary",)),
        name="absorb",
    )(w_uk, w_uq_heads)


def _inproj_kernel(x_ref, g_ref, w_ref, o_ref):
    x = x_ref[...]
    ms = jnp.mean(x * x, axis=-1, keepdims=True)
    xn = x * lax.rsqrt(ms + NORM_EPS) * g_ref[...]
    o_ref[...] = _dot(xn.astype(BF16), w_ref[...])


def _inproj(x2, norm_g, w_all, tm):
    n, d = x2.shape
    return pl.pallas_call(
        _inproj_kernel,
        grid=(n // tm,),
        in_specs=[pl.BlockSpec((tm, d), lambda i: (i, 0)),
                  pl.BlockSpec((1, d), lambda i: (0, 0)),
                  pl.BlockSpec((d, P_COLS), lambda i: (0, 0))],
        out_specs=pl.BlockSpec((tm, P_COLS), lambda i: (i, 0)),
        out_shape=jax.ShapeDtypeStruct((n, P_COLS), F32),
        compiler_params=_params(("parallel",)),
        name="inproj",
    )(x2, norm_g, w_all)


def _rwkv_kernel(rkv_ref, sm_ref, gate_ref, mu_rkv_ref, mu_sm_ref, w0_ref, wup_ref, a0_ref,
                 aup_ref, kkw_ref, ka_ref, rk_ref, gng_ref, gnb_ref, bd_ref, tri_ref,
                 o_ref,
                 carry_rkv, carry_sm, s_scr, n_scr, r_scr, u_scr, k_scr, v_scr, wc_scr, bv_scr,
                 t_scr, ankv_scr, ar_scr, *, tb):
    t = pl.program_id(1)
    W = RWKV_WIDTH

    @pl.when(t == 0)
    def _():
        carry_rkv[...] = jnp.zeros_like(carry_rkv)
        carry_sm[...] = jnp.zeros_like(carry_sm)
        s_scr[...] = jnp.zeros_like(s_scr)

    bd = bd_ref[...]
    half = W // 2

    def head_sum(x):
        st = jnp.concatenate([x[:, :half], x[:, half:]], axis=0)
        res = _dot(st.astype(BF16), bd)
        return jnp.concatenate([res[:CHUNK], res[CHUNK:]], axis=1)

    def shifted(ref, carry, c, rows):
        cur = ref[rows, :]
        first = carry[0:1, :] if c == 0 else ref[c * CHUNK - 1:c * CHUNK, :]
        row = lax.broadcasted_iota(I32, (CHUNK, 1), 0)
        return cur, jnp.where(row == 0, first, pltpu.roll(cur, 1, 0))

    def prep(c):
        rows = pl.ds(c * CHUNK, CHUNK)
        p, prev = shifted(rkv_ref, carry_rkv, c, rows)
        xs = p + (prev - p) * mu_rkv_ref[...]
        r = xs[:, 0:W]
        k = xs[:, W:2 * W]
        v = xs[:, 2 * W:3 * W]
        ps, prev_s = shifted(sm_ref, carry_sm, c, rows)
        xl = ps + (prev_s - ps) * mu_sm_ref[...]
        zw = w0_ref[...] + _dot(jnp.tanh(xl).astype(BF16), wup_ref[...])
        za = a0_ref[...] + _dot(xl.astype(BF16), aup_ref[...])
        yield
        logw = -np.float32(np.exp(-0.5)) * _sigmoid(zw)
        a = _sigmoid(za)
        kk = k * kkw_ref[...]
        ss = head_sum(kk * kk)
        hi, mid, lo = _split3(logw)
        tri = tri_ref[...]
        cum = _dot(tri, hi) + _dot(tri, mid) + _dot(tri, lo)
        yield
        kk = kk * (1.0 / jnp.maximum(jnp.sqrt(ss), 1e-12))
        k2 = k * (1.0 + (a - 1.0) * ka_ref[...])
        bonus = head_sum(r * k2 * rk_ref[...])
        wt = jnp.exp(cum)
        wi = jnp.exp(-cum)
        wx = jnp.exp(cum - logw)
        n_scr[rows, :] = (-kk * wx).astype(BF16)
        r_scr[rows, :] = (r * wt).astype(BF16)
        u_scr[rows, :] = (kk * a * wi).astype(BF16)
        k_scr[rows, :] = (k2 * wi).astype(BF16)
        v_scr[rows, :] = v.astype(BF16)
        wc_scr[c] = wt[CHUNK - SUBLANES:CHUNK, :]
        yield
        bv_scr[rows, :] = bonus * v

    rowp = lax.broadcasted_iota(I32, (PAIR, PAIR), 0)
    colp = lax.broadcasted_iota(I32, (PAIR, PAIR), 1)
    same_blk = (rowp >= HEAD_DIM) == (colp >= HEAD_DIM)
    strict_lo = (colp % HEAD_DIM) < (rowp % HEAD_DIM)
    m_abd = same_blk & strict_lo
    m_ank = jnp.logical_not(same_blk) & strict_lo
    rowc = lax.broadcasted_iota(I32, (CHUNK, PAIR), 0)
    colc = lax.broadcasted_iota(I32, (CHUNK, PAIR), 1)
    incl_lo2 = (lax.broadcasted_iota(I32, (CHUNK, 2 * PAIR), 1) % HEAD_DIM) <= \
        lax.broadcasted_iota(I32, (CHUNK, 2 * PAIR), 0)
    head0_c = colc < HEAD_DIM
    head0_p = colp < HEAD_DIM

    pairs = range(RWKV_HEADS // 2)
    lanes = [slice(g * PAIR, (g + 1) * PAIR) for g in pairs]
    eye = (rowp == colp).astype(F32)

    def phase_a(chunks):
        chains = [(c, g) for c in chunks for g in pairs]
        gtop = []
        for c, g in chains:
            rows = pl.ds(c * CHUNK, CHUNK)
            sl = lanes[g]
            nr = jnp.concatenate([n_scr[rows, sl], r_scr[rows, sl]], axis=0)
            uu = u_scr[rows, sl]
            kc = k_scr[rows, sl]
            zero = jnp.zeros_like(nr)
            rhs = jnp.concatenate(
                [jnp.where(head0_p, jnp.concatenate([uu, kc], axis=0), zero),
                 jnp.where(head0_p, zero, jnp.concatenate([kc, uu], axis=0))], axis=0)
            gg = _dot_nt(nr, rhs)
            ar_scr[c, g] = jnp.where(incl_lo2, gg[CHUNK:PAIR], 0.0).astype(BF16)
            gtop.append(jnp.concatenate([gg[0:CHUNK, 0:PAIR], gg[0:CHUNK, PAIR:]], axis=0))
        yield
        for i, (c, g) in enumerate(chains):
            vv = v_scr[pl.ds(c * CHUNK, CHUNK), lanes[g]]
            ank = jnp.where(m_ank, gtop[i], 0.0).astype(BF16)
            ankv_scr[c, g] = _dot(ank, jnp.concatenate([vv, vv], axis=0))
        n = range(len(chains))
        pw = [jnp.where(m_abd, gtop[i], 0.0) for i in n]
        tm = [eye + pw[i] for i in n]
        pwb = [x.astype(BF16) for x in pw]
        pw = [_dot(x, x) for x in pwb]
        yield
        for _ in range(4):
            pwb = [x.astype(BF16) for x in pw]
            both = [_dot(pwb[i], jnp.concatenate([pwb[i], tm[i].astype(BF16)], axis=1))
                    for i in n]
            pw = [x[:, 0:PAIR] for x in both]
            tm = [tm[i] + both[i][:, PAIR:] for i in n]
            yield
        tm = [tm[i] + _dot(pw[i].astype(BF16), tm[i].astype(BF16)) for i in n]
        for i, (c, g) in enumerate(chains):
            t_scr[c, g] = tm[i].astype(BF16)

    def phase_b(c):
        rows = pl.ds(c * CHUNK, CHUNK)
        wc_all = wc_scr[c][SUBLANES - 1:SUBLANES, :]
        s_old = [s_scr[g] for g in pairs]
        nrh = [_dot_nt(jnp.concatenate([n_scr[rows, lanes[g]], r_scr[rows, lanes[g]]], axis=0),
                       s_old[g].astype(BF16)) for g in pairs]
        yield
        zf = []
        for g in pairs:
            nh = nrh[g][0:CHUNK]
            bf = jnp.concatenate([nh, nh], axis=0) + ankv_scr[c, g]
            zf.append(_dot(t_scr[c, g], bf.astype(BF16)))
        yield
        ys = []
        for g in pairs:
            sl = lanes[g]
            vv = v_scr[rows, sl]
            zb = jnp.where(head0_c, zf[g][0:CHUNK], zf[g][CHUNK:PAIR]).astype(BF16)
            zv = jnp.concatenate([zb, vv], axis=0)
            vz = jnp.concatenate([vv, zb], axis=0)
            zero = jnp.zeros_like(zv)
            rhs = jnp.concatenate([jnp.where(head0_p, zv, zero), jnp.where(head0_p, zero, vz)],
                                  axis=0)
            ys.append(nrh[g][CHUNK:PAIR] + _dot(ar_scr[c, g], rhs))
            uk = jnp.concatenate([u_scr[rows, sl], k_scr[rows, sl]], axis=0)
            s_new = (s_old[g] + _dot_tn(zv, uk)) * wc_all[:, sl]
            s_scr[g] = jnp.where(same_blk, s_new, 0.0)
        yield
        y = jnp.concatenate(ys, axis=1)
        inv_d = np.float32(1.0 / HEAD_DIM)
        yc = y - head_sum(y) * inv_d
        yield
        var = head_sum(yc * yc) * inv_d
        y = yc * lax.rsqrt(var + GN_EPS) * gng_ref[...] + gnb_ref[...] + bv_scr[rows, :]
        gt = gate_ref[rows, :]
        o_ref[rows, :] = (y * (gt * _sigmoid(gt))).astype(o_ref.dtype)

    def in_step(gens):
        gens = list(gens)
        while gens:
            gens = [g for g in gens if next(g, True) is None]
            if gens:
                yield

    def in_turn(gens):
        for g in gens:
            yield from g

    nchunks = tb // CHUNK
    groups = [list(range(c0, c0 + A_GROUP)) for c0 in range(0, nchunks, A_GROUP)]
    for s in range(len(groups) + 2):
        streams = []
        if s - 2 >= 0:
            streams.append(in_turn(phase_b(c) for c in groups[s - 2]))
        if 0 <= s - 1 < len(groups):
            streams.append(phase_a(groups[s - 1]))
        if s < len(groups):
            streams.append(in_step(prep(c) for c in groups[s]))
        while streams:
            streams = [g for g in streams if next(g, True) is None]
    carry_rkv[0:1, :] = rkv_ref[tb - 1:tb, :]
    carry_sm[0:1, :] = sm_ref[tb - 1:tb, :]


def _rwkv(p, b, t, tb, consts):
    nt = t // tb
    W = RWKV_WIDTH
    row_spec = lambda width, cb: pl.BlockSpec((tb, width), lambda bi, ti: (bi * nt + ti, cb))
    full = lambda arr: pl.BlockSpec(arr.shape, lambda bi, ti: (0,) * arr.ndim)
    in_specs = [row_spec(3 * W, C_RKV // (3 * W)), row_spec(LANES, C_SM // LANES),
                row_spec(W, C_GR // W)] + [full(c) for c in consts]
    return pl.pallas_call(
        functools.partial(_rwkv_kernel, tb=tb),
        grid=(b, nt),
        in_specs=in_specs,
        out_specs=pl.BlockSpec((tb, W), lambda bi, ti: (bi * nt + ti, 0)),
        out_shape=jax.ShapeDtypeStruct((b * t, W), BF16),
        scratch_shapes=[pltpu.VMEM((SUBLANES, 3 * W), F32), pltpu.VMEM((SUBLANES, LANES), F32),
                        pltpu.VMEM((RWKV_HEADS // 2, PAIR, PAIR), F32),
                        pltpu.VMEM((tb, W), BF16), pltpu.VMEM((tb, W), BF16),
                        pltpu.VMEM((tb, W), BF16), pltpu.VMEM((tb, W), BF16),
                        pltpu.VMEM((tb, W), BF16),
                        pltpu.VMEM((tb // CHUNK, SUBLANES, W), F32),
                        pltpu.VMEM((tb, W), F32),
                        pltpu.VMEM((tb // CHUNK, RWKV_HEADS // 2, PAIR, PAIR), BF16),
                        pltpu.VMEM((tb // CHUNK, RWKV_HEADS // 2, PAIR, PAIR), F32),
                        pltpu.VMEM((tb // CHUNK, RWKV_HEADS // 2, CHUNK, 2 * PAIR), BF16)],
        compiler_params=_params(("parallel", "arbitrary")),
        name="rwkv",
    )(p, p, p, *consts)


def _dsaprep_kernel(qd_ref, kv_ref, sm_ref, wi_ref, qg_ref, kvg_ref, lng_ref, lnb_ref,
                    wqidx_ref, wabs_ref, wuv_ref,
                    ckv_ref, kidx_ref, vt_ref, qidx_ref, qabs_ref, wit_ref, *, tb):
    nq = tb // QB
    qd = qd_ref[...]
    cq = qd * lax.rsqrt(jnp.mean(qd * qd, axis=-1, keepdims=True) + NORM_EPS) * qg_ref[...]
    cqb = cq.astype(BF16)
    kv = kv_ref[...]
    ckv = kv * lax.rsqrt(jnp.mean(kv * kv, axis=-1, keepdims=True) + NORM_EPS) * kvg_ref[...]
    ckvb = ckv.astype(BF16)

    sm = sm_ref[...]
    lane = lax.broadcasted_iota(I32, sm.shape, 1)
    hi_half = lane >= IDX_DIM
    inv_d = np.float32(1.0 / IDX_DIM)
    mu = jnp.sum(jnp.where(hi_half, sm, 0.0), axis=-1, keepdims=True) * inv_d
    xc = jnp.where(hi_half, sm - mu, 0.0)
    var = jnp.sum(xc * xc, axis=-1, keepdims=True) * inv_d
    kidx = (xc * lax.rsqrt(var + LN_EPS) * lng_ref[...] + lnb_ref[...]).astype(BF16)

    vt = _dot_nt(wuv_ref[...], ckvb).astype(BF16)
    ones = jnp.ones((VROWS - HEAD_DIM, tb), BF16)
    vt = jnp.concatenate(
        [x for h in range(ATTN_HEADS) for x in (vt[h * HEAD_DIM:(h + 1) * HEAD_DIM], ones)],
        axis=0)
    wit = jnp.transpose(wi_ref[...])[0:SUBLANES, :] * np.float32(IDX_HEADS ** -0.5)
    for j in range(tb // KB):
        kidx_ref[0, j] = kidx[j * KB:(j + 1) * KB, :]
    for j in range(nq):
        ckv_ref[0, j] = ckvb[j * QB:(j + 1) * QB, :]
        vt_ref[0, j] = vt[:, j * QB:(j + 1) * QB]
    wit_ref[0] = wit
    for h in range(IDX_HEADS):
        res = _dot_nt(wqidx_ref[h], cqb).astype(BF16)
        for j in range(nq):
            qidx_ref[0, :, (j * IDX_HEADS + h) * QB:(j * IDX_HEADS + h + 1) * QB] = \
                res[:, j * QB:(j + 1) * QB]
    for h in range(ATTN_HEADS):
        res = _dot_nt(wabs_ref[h].astype(BF16), cqb).astype(BF16)
        for j in range(nq):
            qabs_ref[0, :, (j * ATTN_HEADS + h) * QB:(j * ATTN_HEADS + h + 1) * QB] = \
                res[:, j * QB:(j + 1) * QB]


def _dsaprep(p, b, t, tb, consts):
    nt = t // tb
    nb = t // QB
    nq = tb // QB
    row_spec = lambda width, cb: pl.BlockSpec((tb, width), lambda bi, ti: (bi * nt + ti, cb))
    full = lambda arr: pl.BlockSpec(arr.shape, lambda bi, ti: (0,) * arr.ndim)
    in_specs = [row_spec(Q_RANK, C_QD // Q_RANK), row_spec(KV_RANK, C_KV // KV_RANK),
                row_spec(LANES, C_SM // LANES), row_spec(LANES, C_WI // LANES)] + \
               [full(c) for c in consts]
    nkb = t // KB
    nk = tb // KB
    out_shape = [jax.ShapeDtypeStruct((b, nb, QB, KV_RANK), BF16),
                 jax.ShapeDtypeStruct((b, nkb, KB, LANES), BF16),
                 jax.ShapeDtypeStruct((b, nb, ATTN_HEADS * VROWS, QB), BF16),
                 jax.ShapeDtypeStruct((b, LANES, nb * IDX_HEADS * QB), BF16),
                 jax.ShapeDtypeStruct((b, KV_RANK, nb * ATTN_HEADS * QB), BF16),
                 jax.ShapeDtypeStruct((b, SUBLANES, t), F32)]
    out_specs = [pl.BlockSpec((1, nq, QB, KV_RANK), lambda bi, ti: (bi, ti, 0, 0)),
                 pl.BlockSpec((1, nk, KB, LANES), lambda bi, ti: (bi, ti, 0, 0)),
                 pl.BlockSpec((1, nq, ATTN_HEADS * VROWS, QB), lambda bi, ti: (bi, ti, 0, 0)),
                 pl.BlockSpec((1, LANES, nq * IDX_HEADS * QB), lambda bi, ti: (bi, 0, ti)),
                 pl.BlockSpec((1, KV_RANK, nq * ATTN_HEADS * QB), lambda bi, ti: (bi, 0, ti)),
                 pl.BlockSpec((1, SUBLANES, tb), lambda bi, ti: (bi, 0, ti))]
    return pl.pallas_call(
        functools.partial(_dsaprep_kernel, tb=tb),
        grid=(b, nt),
        in_specs=in_specs,
        out_specs=out_specs,
        out_shape=out_shape,
        compiler_params=_params(("parallel", "parallel")),
        name="dsaprep",
    )(p, p, p, p, *consts)


INT_MIN = np.int32(-2 ** 31)
KEY_NEG_INF = np.int32(np.array(0xFF800000, np.uint32).view(np.int32) ^ np.int32(0x7FFFFFFF))


def _dsa_kernel(kidx_ref, ckv_ref, vt_ref, qidx_ref, qabs_ref, wi_ref, gate_ref, tri_ref,
                o_ref, sc_scr, acc_scr, *, topk):
    i = pl.program_id(1)
    nkb = i // (KB // QB) + 1
    row = lax.broadcasted_iota(I32, (KB, QB), 0)
    col = lax.broadcasted_iota(I32, (KB, QB), 1)
    qpos = i * QB + col
    wi = wi_ref[0]

    def score_body(j, carry):
        lg = _dot(kidx_ref[0, j], qidx_ref[0])
        sc = jnp.zeros((KB, QB), F32)
        for h in range(IDX_HEADS):
            sc = sc + wi[h:h + 1, :] * jnp.maximum(lg[:, h * QB:(h + 1) * QB], 0.0)
        sc = jnp.where(j * KB + row <= qpos, sc, -jnp.inf)
        sc_scr[2 * j] = sc[0:QB]
        sc_scr[2 * j + 1] = sc[QB:KB]
        return carry

    lax.fori_loop(0, nkb, score_body, 0)

    def count(pred):
        def body(j, accs):
            a0, a1 = accs
            a0 = a0 + _fold8(jnp.where(pred(sc_scr[2 * j]), 1, 0), jnp.add)
            a1 = a1 + _fold8(jnp.where(pred(sc_scr[2 * j + 1]), 1, 0), jnp.add)
            return a0, a1
        zero = jnp.zeros((SUBLANES, QB), I32)
        a0, a1 = lax.fori_loop(0, nkb, body, (zero, zero))
        return jnp.sum(a0 + a1, axis=0, keepdims=True)

    def key_value(key):
        return lax.bitcast_convert_type(jnp.where(key < 0, key ^ np.int32(0x7FFFFFFF), key), F32)

    thr_key = jnp.full((1, QB), INT_MIN, I32)
    for bit in range(31, -1, -1):
        cand = jnp.zeros((1, QB), I32) if bit == 31 else thr_key | np.int32(1 << bit)
        cand_val = key_value(cand)
        cnt = count(lambda s, v=cand_val: s >= v)
        thr_key = jnp.where(cnt >= topk, cand, thr_key)
    thr = jnp.where(thr_key <= KEY_NEG_INF, -jnp.inf, key_value(thr_key))
    need = (topk - count(lambda s: s > thr)).astype(F32)

    tri = tri_ref[...]

    acc_scr[...] = jnp.zeros_like(acc_scr)

    def attend(c, carry):
        run, ms = carry
        key = sc_scr[c]
        eq = key == thr
        eqf = jnp.where(eq, 1.0, 0.0)
        before = _dot(tri, eqf.astype(BF16)) + run
        sel = (key > thr) | (eq & (before < need))
        sel = sel & (c * QB + lax.broadcasted_iota(I32, (QB, QB), 0) <=
                     i * QB + lax.broadcasted_iota(I32, (QB, QB), 1))
        bias = jnp.where(sel, 0.0, -jnp.inf)
        run = run + _colsum(eqf)
        heads = range(ATTN_HEADS)
        s = _dot(ckv_ref[0, c], qabs_ref[0])
        sh = [s[:, h * QB:(h + 1) * QB] + bias for h in heads]
        new_ms = [jnp.maximum(ms[h], _colmax(sh[h])) for h in heads]
        shift = [jnp.where(m == -jnp.inf, 0.0, m) for m in new_ms]
        ps = [jnp.exp2((sh[h] - shift[h]).astype(BF16)) for h in heads]
        outs = [_dot(vt_ref[0, c, h * VROWS:(h + 1) * VROWS, :], ps[h]) for h in heads]
        scales = [jnp.broadcast_to(jnp.exp2(ms[h] - shift[h]), (VROWS, QB)) for h in heads]
        acc_scr[...] = acc_scr[...] * jnp.concatenate(scales, axis=0) + \
            jnp.concatenate(outs, axis=0)
        return run, tuple(new_ms)

    ninf = jnp.full((1, QB), -jnp.inf, F32)
    lax.fori_loop(0, i + 1, attend,
                  (jnp.zeros((1, QB), F32), tuple(ninf for _ in range(ATTN_HEADS))))

    o = jnp.concatenate(
        [acc_scr[h * VROWS:h * VROWS + HEAD_DIM, :] *
         (1.0 / acc_scr[h * VROWS + HEAD_DIM:h * VROWS + HEAD_DIM + 1, :])
         for h in range(ATTN_HEADS)], axis=0)
    o = jnp.transpose(o)
    gt = gate_ref[...]
    o_ref[...] = (o * (gt * _sigmoid(gt))).astype(o_ref.dtype)


def _dsa(p, prep, tri, b, t, topk):
    nb = t // QB
    ckv, kidx, vt, qidx, qabs, wit = prep
    nkb = t // KB
    in_specs = [pl.BlockSpec((1, nkb, KB, LANES), lambda bi, qi: (bi, 0, 0, 0)),
                pl.BlockSpec((1, nb, QB, KV_RANK), lambda bi, qi: (bi, 0, 0, 0)),
                pl.BlockSpec((1, nb, ATTN_HEADS * VROWS, QB), lambda bi, qi: (bi, 0, 0, 0)),
                pl.BlockSpec((1, LANES, IDX_HEADS * QB), lambda bi, qi: (bi, 0, qi)),
                pl.BlockSpec((1, KV_RANK, ATTN_HEADS * QB), lambda bi, qi: (bi, 0, qi)),
                pl.BlockSpec((1, SUBLANES, QB), lambda bi, qi: (bi, 0, qi)),
                pl.BlockSpec((QB, ATTN_WIDTH), lambda bi, qi: (bi * nb + qi, C_GA // ATTN_WIDTH)),
                pl.BlockSpec((QB, QB), lambda bi, qi: (0, 0))]
    return pl.pallas_call(
        functools.partial(_dsa_kernel, topk=topk),
        grid=(b, nb),
        in_specs=in_specs,
        out_specs=pl.BlockSpec((QB, ATTN_WIDTH), lambda bi, qi: (bi * nb + qi, 0)),
        out_shape=jax.ShapeDtypeStruct((b * t, ATTN_WIDTH), BF16),
        scratch_shapes=[pltpu.VMEM((nb, QB, QB), F32),
                        pltpu.VMEM((ATTN_HEADS * VROWS, QB), F32)],
        compiler_params=_params(("parallel", "arbitrary")),
        name="dsa",
    )(kidx, ckv, vt, qidx, qabs, wit, p, tri)


def _outproj_kernel(yr_ref, ya_ref, x_ref, w_ref, g_ref, o_ref):
    w = w_ref[...]
    y = _dot(yr_ref[...], w[0:RWKV_WIDTH]) + _dot(ya_ref[...], w[RWKV_WIDTH:])
    z = x_ref[...] + y
    ms = jnp.mean(z * z, axis=-1, keepdims=True)
    o_ref[...] = z * lax.rsqrt(ms + NORM_EPS) * g_ref[...]


def _outproj(yr, ya, x2, w_out, final_g, tm):
    n, d = x2.shape
    return pl.pallas_call(
        _outproj_kernel,
        grid=(n // tm,),
        in_specs=[pl.BlockSpec((tm, RWKV_WIDTH), lambda i: (i, 0)),
                  pl.BlockSpec((tm, ATTN_WIDTH), lambda i: (i, 0)),
                  pl.BlockSpec((tm, d), lambda i: (i, 0)),
                  pl.BlockSpec(w_out.shape, lambda i: (0, 0)),
                  pl.BlockSpec((1, d), lambda i: (0, 0))],
        out_specs=pl.BlockSpec((tm, d), lambda i: (i, 0)),
        out_shape=jax.ShapeDtypeStruct((n, d), F32),
        compiler_params=_params(("parallel",)),
        name="outproj",
    )(yr, ya, x2, w_out, final_g)


class _BlockRows(NamedTuple):
    inproj_rows: int
    rwkv_rows: int
    outproj_rows: int
    dsaprep_rows: int


def _block_rows(b, t):
    n = b * t
    assert t % KB == 0 and t % (A_GROUP * CHUNK) == 0
    pick = lambda want, total: want if total % want == 0 else KB
    return _BlockRows(inproj_rows=pick(512, n), rwkv_rows=pick(512, t),
                      outproj_rows=pick(1024, n), dsaprep_rows=pick(1024, t))


def _pad_rows(w, start, total):
    return jnp.zeros((total, w.shape[1]), w.dtype).at[start:start + w.shape[0]].set(w)


def _layer(x2, b, t, norm_g, w_in, mu_shift, w0, w_up, a0, a_up, k_k, k_a, r_k, gn_g, gn_b,
           q_norm_g, kv_norm_g, w_uq, w_uk, w_uv, w_qidx, kidx_g, kidx_b, w_out, final_g):
    W = RWKV_WIDTH
    d = x2.shape[1]
    o_r, o_k, o_v = 0, W, 2 * W
    o_wd = 3 * W
    o_ad = o_wd + DECAY_LORA
    o_gr = o_ad + AAA_LORA
    o_qd = o_gr + W
    o_kv = o_qd + Q_RANK
    o_ki = o_kv + KV_RANK
    o_wi = o_ki + IDX_DIM
    o_ga = o_wi + IDX_HEADS
    cols = lambda s, n: w_in[:, s:s + n]
    w_all = jnp.concatenate(
        [cols(o_r, 3 * W), cols(o_gr, W), cols(o_ga, ATTN_WIDTH), cols(o_qd, Q_RANK),
         cols(o_kv, KV_RANK), cols(o_wd, DECAY_LORA), cols(o_ad, AAA_LORA), cols(o_ki, IDX_DIM),
         cols(o_wi, IDX_HEADS), jnp.zeros((d, LANES - IDX_HEADS), w_in.dtype)],
        axis=1).astype(BF16)

    row2 = lambda v: v.reshape(1, -1).astype(F32)
    mu_rkv = row2(mu_shift[0:3 * W])
    mu_sm = row2(jnp.concatenate([mu_shift[3 * W:], jnp.zeros((IDX_DIM,), F32)]))
    wup_pad = _pad_rows(w_up, 0, LANES).astype(BF16)
    aup_pad = _pad_rows(a_up, DECAY_LORA, LANES).astype(BF16)
    hid = np.arange(W // 2) // HEAD_DIM
    bd = jnp.asarray((hid[:, None] == hid[None, :]).astype(np.float32), dtype=BF16)

    blk = _block_rows(b, t)
    ti = np.arange(CHUNK)
    tri_r = jnp.asarray((ti[None, :] <= ti[:, None]).astype(np.float32), dtype=BF16)
    rwkv_consts = [mu_rkv, mu_sm, row2(w0), wup_pad, row2(a0), aup_pad, row2(k_k), row2(k_a),
                   row2(r_k), row2(gn_g), row2(gn_b), bd, tri_r]

    wq_t = jnp.transpose(w_qidx.reshape(Q_RANK, IDX_HEADS, IDX_DIM), (1, 2, 0))
    wq_t = jnp.concatenate([jnp.zeros_like(wq_t), wq_t], axis=1).astype(BF16)
    w_uq_heads = jnp.transpose(w_uq.reshape(Q_RANK, ATTN_HEADS, HEAD_DIM), (1, 0, 2))
    wabs_t = _absorb(w_uk, w_uq_heads)
    wuv_t = jnp.transpose(w_uv, (0, 2, 1)).reshape(ATTN_WIDTH, KV_RANK).astype(BF16)
    lng = row2(jnp.concatenate([jnp.zeros((IDX_DIM,), F32), kidx_g]))
    lnb = row2(jnp.concatenate([jnp.zeros((IDX_DIM,), F32), kidx_b]))
    prep_consts = [row2(q_norm_g), row2(kv_norm_g), lng, lnb, wq_t, wabs_t, wuv_t]
    ki = np.arange(QB)
    tri_q = jnp.asarray((ki[None, :] < ki[:, None]).astype(np.float32), dtype=BF16)

    p = _inproj(x2, row2(norm_g), w_all, blk.inproj_rows)
    y_r = _rwkv(p, b, t, blk.rwkv_rows, rwkv_consts)
    prep = _dsaprep(p, b, t, blk.dsaprep_rows, prep_consts)
    y_a = _dsa(p, prep, tri_q, b, t, min(TOPK_MAX, t // 4))
    return _outproj(y_r, y_a, x2, w_out.astype(BF16), row2(final_g), blk.outproj_rows)


def kernel(x, norm_g, w_in, mu_shift, w0, w_up, a0, a_up, k_k, k_a, r_k, gn_g, gn_b,
           q_norm_g, kv_norm_g, w_uq, w_uk, w_uv, w_qidx, kidx_g, kidx_b, w_out, final_g):
    b, t, d = x.shape
    assert norm_g.shape[0] == 1, "single-layer problem"
    out = _layer(x.reshape(b * t, d), b, t, norm_g[0], w_in[0], mu_shift[0], w0[0], w_up[0],
                 a0[0], a_up[0], k_k[0], k_a[0], r_k[0], gn_g[0], gn_b[0], q_norm_g[0],
                 kv_norm_g[0], w_uq[0], w_uk[0], w_uv[0], w_qidx[0], kidx_g[0], kidx_b[0],
                 w_out[0], final_g)
    return out.reshape(b, t, d)
```

```python
import functools
from typing import NamedTuple

import numpy as np
import jax
import jax.numpy as jnp
from jax import lax
from jax.experimental import pallas as pl
from jax.experimental.pallas import tpu as pltpu

F32 = jnp.float32
BF16 = jnp.bfloat16
I32 = jnp.int32

HEAD_DIM = 64
RWKV_HEADS = 8
RWKV_WIDTH = RWKV_HEADS * HEAD_DIM
DECAY_LORA = 32
AAA_LORA = 32
ATTN_HEADS = 8
ATTN_WIDTH = ATTN_HEADS * HEAD_DIM
Q_RANK = 256
KV_RANK = 128
IDX_HEADS = 4
IDX_DIM = 64
TOPK_MAX = 256
NORM_EPS = 1e-6
LN_EPS = 1e-5
GN_EPS = 64e-5

LANES = 128
SUBLANES = 8
BF16_ROWS = 16
QB = 256
KB = 2 * QB
CHUNK = 64
PAIR = 2 * HEAD_DIM
A_GROUP = 2
VROWS = HEAD_DIM + 16

C_RKV = 0
C_GR = 3 * RWKV_WIDTH
C_GA = C_GR + RWKV_WIDTH
C_QD = C_GA + ATTN_WIDTH
C_KV = C_QD + Q_RANK
C_SM = C_KV + KV_RANK
C_WI = C_SM + LANES
P_COLS = C_WI + LANES

VMEM_LIMIT = 56 * 1024 * 1024


def _params(sem):
    return pltpu.CompilerParams(dimension_semantics=sem, vmem_limit_bytes=VMEM_LIMIT)


def _dot(a, b):
    return jnp.dot(a, b, preferred_element_type=F32)


def _dot_nt(a, b):
    return lax.dot_general(a, b, (((1,), (1,)), ((), ())), preferred_element_type=F32)


def _dot_tn(a, b):
    return lax.dot_general(a, b, (((0,), (0,)), ((), ())), preferred_element_type=F32)


def _split2(x):
    hi = x.astype(BF16)
    lo = (x - hi.astype(F32)).astype(BF16)
    return hi, lo


def _split3(x):
    hi = x.astype(BF16)
    r1 = x - hi.astype(F32)
    mid = r1.astype(BF16)
    lo = (r1 - mid.astype(F32)).astype(BF16)
    return hi, mid, lo


def _dot_exact_rhs(x, m_bf16, parts=2):
    if parts == 2:
        hi, lo = _split2(x)
        return _dot(hi, m_bf16) + _dot(lo, m_bf16)
    hi, mid, lo = _split3(x)
    return _dot(hi, m_bf16) + _dot(mid, m_bf16) + _dot(lo, m_bf16)


def _fold8(x, op, rows=SUBLANES):
    acc = x[0:rows]
    for j in range(1, x.shape[0] // rows):
        acc = op(acc, x[j * rows:(j + 1) * rows])
    return acc


def _colsum(x):
    return jnp.sum(_fold8(x, jnp.add), axis=0, keepdims=True)


def _colmax(x):
    return jnp.max(_fold8(x, jnp.maximum), axis=0, keepdims=True)


def _sigmoid(x):
    return 1.0 / (1.0 + jnp.exp(-x))


def _absorb_kernel(wuk_ref, wuq_ref, o_ref):
    a = wuk_ref[0]
    b = wuq_ref[0]
    ah, al = _split2(a)
    bh, bl = _split2(b)
    acc = _dot_nt(ah, bh) + _dot_nt(ah, bl) + _dot_nt(al, bh)
    o_ref[0] = acc * np.float32(HEAD_DIM ** -0.5 * np.log2(np.e))


def _absorb(w_uk, w_uq_heads):
    return pl.pallas_call(
        _absorb_kernel,
        grid=(ATTN_HEADS,),
        in_specs=[pl.BlockSpec((1, KV_RANK, HEAD_DIM), lambda h: (h, 0, 0)),
                  pl.BlockSpec((1, Q_RANK, HEAD_DIM), lambda h: (h, 0, 0))],
        out_specs=pl.BlockSpec((1, KV_RANK, Q_RANK), lambda h: (h, 0, 0)),
        out_shape=jax.ShapeDtypeStruct((ATTN_HEADS, KV_RANK, Q_RANK), F32),
        compiler_params=_params(("arbitrary",)),
        name="absorb",
    )(w_uk, w_uq_heads)


def _inproj_kernel(x_ref, g_ref, w_ref, o_ref):
    x = x_ref[...]
    ms = jnp.mean(x * x, axis=-1, keepdims=True)
    xn = x * lax.rsqrt(ms + NORM_EPS) * g_ref[...]
    o_ref[...] = _dot(xn.astype(BF16), w_ref[...])


def _inproj(x2, norm_g, w_all, tm):
    n, d = x2.shape
    return pl.pallas_call(
        _inproj_kernel,
        grid=(n // tm,),
        in_specs=[pl.BlockSpec((tm, d), lambda i: (i, 0)),
                  pl.BlockSpec((1, d), lambda i: (0, 0)),
                  pl.BlockSpec((d, P_COLS), lambda i: (0, 0))],
        out_specs=pl.BlockSpec((tm, P_COLS), lambda i: (i, 0)),
        out_shape=jax.ShapeDtypeStruct((n, P_COLS), F32),
        compiler_params=_params(("parallel",)),
        name="inproj",
    )(x2, norm_g, w_all)


def _rwkv_kernel(rkv_ref, sm_ref, gate_ref, mu_rkv_ref, mu_sm_ref, w0_ref, wup_ref, a0_ref,
                 aup_ref, kkw_ref, ka_ref, rk_ref, gng_ref, gnb_ref, bd_ref, tri_ref,
                 o_ref,
                 carry_rkv, carry_sm, s_scr, n_scr, r_scr, u_scr, k_scr, v_scr, wc_scr, bv_scr,
                 t_scr, ankv_scr, ar_scr, *, tb):
    t = pl.program_id(1)
    W = RWKV_WIDTH

    @pl.when(t == 0)
    def _():
        carry_rkv[...] = jnp.zeros_like(carry_rkv)
        carry_sm[...] = jnp.zeros_like(carry_sm)
        s_scr[...] = jnp.zeros_like(s_scr)

    bd = bd_ref[...]
    half = W // 2

    def head_sum(x):
        st = jnp.concatenate([x[:, :half], x[:, half:]], axis=0)
        res = _dot(st.astype(BF16), bd)
        return jnp.concatenate([res[:CHUNK], res[CHUNK:]], axis=1)

    def shifted(ref, carry, c, rows):
        cur = ref[rows, :]
        first = carry[0:1, :] if c == 0 else ref[c * CHUNK - 1:c * CHUNK, :]
        row = lax.broadcasted_iota(I32, (CHUNK, 1), 0)
        return cur, jnp.where(row == 0, first, pltpu.roll(cur, 1, 0))

    def prep(c):
        rows = pl.ds(c * CHUNK, CHUNK)
        p, prev = shifted(rkv_ref, carry_rkv, c, rows)
        xs = p + (prev - p) * mu_rkv_ref[...]
        r = xs[:, 0:W]
        k = xs[:, W:2 * W]
        v = xs[:, 2 * W:3 * W]
        ps, prev_s = shifted(sm_ref, carry_sm, c, rows)
        xl = ps + (prev_s - ps) * mu_sm_ref[...]
        zw = w0_ref[...] + _dot(jnp.tanh(xl).astype(BF16), wup_ref[...])
        za = a0_ref[...] + _dot(xl.astype(BF16), aup_ref[...])
        yield
        logw = -np.float32(np.exp(-0.5)) * _sigmoid(zw)
        a = _sigmoid(za)
        kk = k * kkw_ref[...]
        ss = head_sum(kk * kk)
        hi, mid, lo = _split3(logw)
        tri = tri_ref[...]
        cum = _dot(tri, hi) + _dot(tri, mid) + _dot(tri, lo)
        yield
        kk = kk * (1.0 / jnp.maximum(jnp.sqrt(ss), 1e-12))
        k2 = k * (1.0 + (a - 1.0) * ka_ref[...])
        bonus = head_sum(r * k2 * rk_ref[...])
        wt = jnp.exp(cum)
        wi = jnp.exp(-cum)
        wx = jnp.exp(cum - logw)
        n_scr[rows, :] = (-kk * wx).astype(BF16)
        r_scr[rows, :] = (r * wt).astype(BF16)
        u_scr[rows, :] = (kk * a * wi).astype(BF16)
        k_scr[rows, :] = (k2 * wi).astype(BF16)
        v_scr[rows, :] = v.astype(BF16)
        wc_scr[c] = wt[CHUNK - SUBLANES:CHUNK, :]
        yield
        bv_scr[rows, :] = bonus * v

    rowp = lax.broadcasted_iota(I32, (PAIR, PAIR), 0)
    colp = lax.broadcasted_iota(I32, (PAIR, PAIR), 1)
    same_blk = (rowp >= HEAD_DIM) == (colp >= HEAD_DIM)
    strict_lo = (colp % HEAD_DIM) < (rowp % HEAD_DIM)
    m_abd = same_blk & strict_lo
    m_ank = jnp.logical_not(same_blk) & strict_lo
    rowc = lax.broadcasted_iota(I32, (CHUNK, PAIR), 0)
    colc = lax.broadcasted_iota(I32, (CHUNK, PAIR), 1)
    incl_lo2 = (lax.broadcasted_iota(I32, (CHUNK, 2 * PAIR), 1) % HEAD_DIM) <= \
        lax.broadcasted_iota(I32, (CHUNK, 2 * PAIR), 0)
    head0_c = colc < HEAD_DIM
    head0_p = colp < HEAD_DIM

    pairs = range(RWKV_HEADS // 2)
    lanes = [slice(g * PAIR, (g + 1) * PAIR) for g in pairs]
    eye = (rowp == colp).astype(F32)

    def phase_a(chunks):
        chains = [(c, g) for c in chunks for g in pairs]
        gtop = []
        for c, g in chains:
            rows = pl.ds(c * CHUNK, CHUNK)
            sl = lanes[g]
            nr = jnp.concatenate([n_scr[rows, sl], r_scr[rows, sl]], axis=0)
            uu = u_scr[rows, sl]
            kc = k_scr[rows, sl]
            zero = jnp.zeros_like(nr)
            rhs = jnp.concatenate(
                [jnp.where(head0_p, jnp.concatenate([uu, kc], axis=0), zero),
                 jnp.where(head0_p, zero, jnp.concatenate([kc, uu], axis=0))], axis=0)
            gg = _dot_nt(nr, rhs)
            ar_scr[c, g] = jnp.where(incl_lo2, gg[CHUNK:PAIR], 0.0).astype(BF16)
            gtop.append(jnp.concatenate([gg[0:CHUNK, 0:PAIR], gg[0:CHUNK, PAIR:]], axis=0))
        yield
        for i, (c, g) in enumerate(chains):
            vv = v_scr[pl.ds(c * CHUNK, CHUNK), lanes[g]]
            ank = jnp.where(m_ank, gtop[i], 0.0).astype(BF16)
            ankv_scr[c, g] = _dot(ank, jnp.concatenate([vv, vv], axis=0))
        n = range(len(chains))
        pw = [jnp.where(m_abd, gtop[i], 0.0) for i in n]
        tm = [eye + pw[i] for i in n]
        pwb = [x.astype(BF16) for x in pw]
        pw = [_dot(x, x) for x in pwb]
        yield
        for _ in range(4):
            pwb = [x.astype(BF16) for x in pw]
            both = [_dot(pwb[i], jnp.concatenate([pwb[i], tm[i].astype(BF16)], axis=1))
                    for i in n]
            pw = [x[:, 0:PAIR] for x in both]
            tm = [tm[i] + both[i][:, PAIR:] for i in n]
            yield
        tm = [tm[i] + _dot(pw[i].astype(BF16), tm[i].astype(BF16)) for i in n]
        for i, (c, g) in enumerate(chains):
            t_scr[c, g] = tm[i].astype(BF16)

    def phase_b(c):
        rows = pl.ds(c * CHUNK, CHUNK)
        wc_all = wc_scr[c][SUBLANES - 1:SUBLANES, :]
        s_old = [s_scr[g] for g in pairs]
        nrh = [_dot_nt(jnp.concatenate([n_scr[rows, lanes[g]], r_scr[rows, lanes[g]]], axis=0),
                       s_old[g].astype(BF16)) for g in pairs]
        yield
        zf = []
        for g in pairs:
            nh = nrh[g][0:CHUNK]
            bf = jnp.concatenate([nh, nh], axis=0) + ankv_scr[c, g]
            zf.append(_dot(t_scr[c, g], bf.astype(BF16)))
        yield
        ys = []
        for g in pairs:
            sl = lanes[g]
            vv = v_scr[rows, sl]
            zb = jnp.where(head0_c, zf[g][0:CHUNK], zf[g][CHUNK:PAIR]).astype(BF16)
            zv = jnp.concatenate([zb, vv], axis=0)
            vz = jnp.concatenate([vv, zb], axis=0)
            zero = jnp.zeros_like(zv)
            rhs = jnp.concatenate([jnp.where(head0_p, zv, zero), jnp.where(head0_p, zero, vz)],
                                  axis=0)
            ys.append(nrh[g][CHUNK:PAIR] + _dot(ar_scr[c, g], rhs))
            uk = jnp.concatenate([u_scr[rows, sl], k_scr[rows, sl]], axis=0)
            s_new = (s_old[g] + _dot_tn(zv, uk)) * wc_all[:, sl]
            s_scr[g] = jnp.where(same_blk, s_new, 0.0)
        yield
        y = jnp.concatenate(ys, axis=1)
        inv_d = np.float32(1.0 / HEAD_DIM)
        yc = y - head_sum(y) * inv_d
        yield
        var = head_sum(yc * yc) * inv_d
        y = yc * lax.rsqrt(var + GN_EPS) * gng_ref[...] + gnb_ref[...] + bv_scr[rows, :]
        gt = gate_ref[rows, :]
        o_ref[rows, :] = (y * (gt * _sigmoid(gt))).astype(o_ref.dtype)

    def in_step(gens):
        gens = list(gens)
        while gens:
            gens = [g for g in gens if next(g, True) is None]
            if gens:
                yield

    def in_turn(gens):
        for g in gens:
            yield from g

    nchunks = tb // CHUNK
    groups = [list(range(c0, c0 + A_GROUP)) for c0 in range(0, nchunks, A_GROUP)]
    for s in range(len(groups) + 2):
        streams = []
        if s - 2 >= 0:
            streams.append(in_turn(phase_b(c) for c in groups[s - 2]))
        if 0 <= s - 1 < len(groups):
            streams.append(phase_a(groups[s - 1]))
        if s < len(groups):
            streams.append(in_step(prep(c) for c in groups[s]))
        while streams:
            streams = [g for g in streams if next(g, True) is None]
    carry_rkv[0:1, :] = rkv_ref[tb - 1:tb, :]
    carry_sm[0:1, :] = sm_ref[tb - 1:tb, :]


def _rwkv(p, b, t, tb, consts):
    nt = t // tb
    W = RWKV_WIDTH
    row_spec = lambda width, cb: pl.BlockSpec((tb, width), lambda bi, ti: (bi * nt + ti, cb))
    full = lambda arr: pl.BlockSpec(arr.shape, lambda bi, ti: (0,) * arr.ndim)
    in_specs = [row_spec(3 * W, C_RKV // (3 * W)), row_spec(LANES, C_SM // LANES),
                row_spec(W, C_GR // W)] + [full(c) for c in consts]
    return pl.pallas_call(
        functools.partial(_rwkv_kernel, tb=tb),
        grid=(b, nt),
        in_specs=in_specs,
        out_specs=pl.BlockSpec((tb, W), lambda bi, ti: (bi * nt + ti, 0)),
        out_shape=jax.ShapeDtypeStruct((b * t, W), BF16),
        scratch_shapes=[pltpu.VMEM((SUBLANES, 3 * W), F32), pltpu.VMEM((SUBLANES, LANES), F32),
                        pltpu.VMEM((RWKV_HEADS // 2, PAIR, PAIR), F32),
                        pltpu.VMEM((tb, W), BF16), pltpu.VMEM((tb, W), BF16),
                        pltpu.VMEM((tb, W), BF16), pltpu.VMEM((tb, W), BF16),
                        pltpu.VMEM((tb, W), BF16),
                        pltpu.VMEM((tb // CHUNK, SUBLANES, W), F32),
                        pltpu.VMEM((tb, W), F32),
                        pltpu.VMEM((tb // CHUNK, RWKV_HEADS // 2, PAIR, PAIR), BF16),
                        pltpu.VMEM((tb // CHUNK, RWKV_HEADS // 2, PAIR, PAIR), F32),
                        pltpu.VMEM((tb // CHUNK, RWKV_HEADS // 2, CHUNK, 2 * PAIR), BF16)],
        compiler_params=_params(("parallel", "arbitrary")),
        name="rwkv",
    )(p, p, p, *consts)


def _dsaprep_kernel(qd_ref, kv_ref, sm_ref, wi_ref, qg_ref, kvg_ref, lng_ref, lnb_ref,
                    wqidx_ref, wabs_ref, wuv_ref,
                    ckv_ref, kidx_ref, vt_ref, qidx_ref, qabs_ref, wit_ref, *, tb):
    nq = tb // QB
    qd = qd_ref[...]
    cq = qd * lax.rsqrt(jnp.mean(qd * qd, axis=-1, keepdims=True) + NORM_EPS) * qg_ref[...]
    cqb = cq.astype(BF16)
    kv = kv_ref[...]
    ckv = kv * lax.rsqrt(jnp.mean(kv * kv, axis=-1, keepdims=True) + NORM_EPS) * kvg_ref[...]
    ckvb = ckv.astype(BF16)

    sm = sm_ref[...]
    lane = lax.broadcasted_iota(I32, sm.shape, 1)
    hi_half = lane >= IDX_DIM
    inv_d = np.float32(1.0 / IDX_DIM)
    mu = jnp.sum(jnp.where(hi_half, sm, 0.0), axis=-1, keepdims=True) * inv_d
    xc = jnp.where(hi_half, sm - mu, 0.0)
    var = jnp.sum(xc * xc, axis=-1, keepdims=True) * inv_d
    kidx = (xc * lax.rsqrt(var + LN_EPS) * lng_ref[...] + lnb_ref[...]).astype(BF16)

    vt = _dot_nt(wuv_ref[...], ckvb).astype(BF16)
    ones = jnp.ones((VROWS - HEAD_DIM, tb), BF16)
    vt = jnp.concatenate(
        [x for h in range(ATTN_HEADS) for x in (vt[h * HEAD_DIM:(h + 1) * HEAD_DIM], ones)],
        axis=0)
    wit = jnp.transpose(wi_ref[...])[0:SUBLANES, :] * np.float32(IDX_HEADS ** -0.5)
    for j in range(tb // KB):
        kidx_ref[0, j] = kidx[j * KB:(j + 1) * KB, :]
    for j in range(nq):
        ckv_ref[0, j] = ckvb[j * QB:(j + 1) * QB, :]
        vt_ref[0, j] = vt[:, j * QB:(j + 1) * QB]
    wit_ref[0] = wit
    for h in range(IDX_HEADS):
        res = _dot_nt(wqidx_ref[h], cqb).astype(BF16)
        for j in range(nq):
            qidx_ref[0, :, (j * IDX_HEADS + h) * QB:(j * IDX_HEADS + h + 1) * QB] = \
                res[:, j * QB:(j + 1) * QB]
    for h in range(ATTN_HEADS):
        res = _dot_nt(wabs_ref[h].astype(BF16), cqb).astype(BF16)
        for j in range(nq):
            qabs_ref[0, :, (j * ATTN_HEADS + h) * QB:(j * ATTN_HEADS + h + 1) * QB] = \
                res[:, j * QB:(j + 1) * QB]


def _dsaprep(p, b, t, tb, consts):
    nt = t // tb
    nb = t // QB
    nq = tb // QB
    row_spec = lambda width, cb: pl.BlockSpec((tb, width), lambda bi, ti: (bi * nt + ti, cb))
    full = lambda arr: pl.BlockSpec(arr.shape, lambda bi, ti: (0,) * arr.ndim)
    in_specs = [row_spec(Q_RANK, C_QD // Q_RANK), row_spec(KV_RANK, C_KV // KV_RANK),
                row_spec(LANES, C_SM // LANES), row_spec(LANES, C_WI // LANES)] + \
               [full(c) for c in consts]
    nkb = t // KB
    nk = tb // KB
    out_shape = [jax.ShapeDtypeStruct((b, nb, QB, KV_RANK), BF16),
                 jax.ShapeDtypeStruct((b, nkb, KB, LANES), BF16),
                 jax.ShapeDtypeStruct((b, nb, ATTN_HEADS * VROWS, QB), BF16),
                 jax.ShapeDtypeStruct((b, LANES, nb * IDX_HEADS * QB), BF16),
                 jax.ShapeDtypeStruct((b, KV_RANK, nb * ATTN_HEADS * QB), BF16),
                 jax.ShapeDtypeStruct((b, SUBLANES, t), F32)]
    out_specs = [pl.BlockSpec((1, nq, QB, KV_RANK), lambda bi, ti: (bi, ti, 0, 0)),
                 pl.BlockSpec((1, nk, KB, LANES), lambda bi, ti: (bi, ti, 0, 0)),
                 pl.BlockSpec((1, nq, ATTN_HEADS * VROWS, QB), lambda bi, ti: (bi, ti, 0, 0)),
                 pl.BlockSpec((1, LANES, nq * IDX_HEADS * QB), lambda bi, ti: (bi, 0, ti)),
                 pl.BlockSpec((1, KV_RANK, nq * ATTN_HEADS * QB), lambda bi, ti: (bi, 0, ti)),
                 pl.BlockSpec((1, SUBLANES, tb), lambda bi, ti: (bi, 0, ti))]
    return pl.pallas_call(
        functools.partial(_dsaprep_kernel, tb=tb),
        grid=(b, nt),
        in_specs=in_specs,
        out_specs=out_specs,
        out_shape=out_shape,
        compiler_params=_params(("parallel", "parallel")),
        name="dsaprep",
    )(p, p, p, p, *consts)


INT_MIN = np.int32(-2 ** 31)
KEY_NEG_INF = np.int32(np.array(0xFF800000, np.uint32).view(np.int32) ^ np.int32(0x7FFFFFFF))


def _dsa_kernel(kidx_ref, ckv_ref, vt_ref, qidx_ref, qabs_ref, wi_ref, gate_ref, tri_ref,
                o_ref, sc_scr, scb_scr, acc_scr, *, topk):
    i = pl.program_id(1)
    nkb = i // (KB // QB) + 1
    row = lax.broadcasted_iota(I32, (KB, QB), 0)
    col = lax.broadcasted_iota(I32, (KB, QB), 1)
    qpos = i * QB + col
    wi = wi_ref[0]

    def score_body(j, carry):
        lg = _dot(kidx_ref[0, j], qidx_ref[0])
        sc = jnp.zeros((KB, QB), F32)
        for h in range(IDX_HEADS):
            sc = sc + wi[h:h + 1, :] * jnp.maximum(lg[:, h * QB:(h + 1) * QB], 0.0)
        sc = jnp.where(j * KB + row <= qpos, sc, -jnp.inf)
        sc_scr[2 * j] = sc[0:QB]
        sc_scr[2 * j + 1] = sc[QB:KB]
        scb = sc.astype(BF16)
        scb_scr[2 * j] = scb[0:QB]
        scb_scr[2 * j + 1] = scb[QB:KB]
        return carry

    lax.fori_loop(0, nkb, score_body, 0)

    def count(pred):
        def body(j, accs):
            a0, a1 = accs
            a0 = a0 + _fold8(jnp.where(pred(sc_scr[2 * j]), 1, 0), jnp.add)
            a1 = a1 + _fold8(jnp.where(pred(sc_scr[2 * j + 1]), 1, 0), jnp.add)
            return a0, a1
        zero = jnp.zeros((SUBLANES, QB), I32)
        a0, a1 = lax.fori_loop(0, nkb, body, (zero, zero))
        return jnp.sum(a0 + a1, axis=0, keepdims=True)

    def key_value(key):
        return lax.bitcast_convert_type(jnp.where(key < 0, key ^ np.int32(0x7FFFFFFF), key), F32)

    def count_bf16(cand):
        one = jnp.ones((QB, QB), BF16)
        zero = jnp.zeros((QB, QB), BF16)

        def body(j, acc):
            for c in (2 * j, 2 * j + 1):
                hits = jnp.where(scb_scr[c] >= cand, one, zero)
                acc = acc + _fold8(hits, jnp.add, rows=BF16_ROWS).astype(F32)
            return acc
        acc = lax.fori_loop(0, nkb, body, jnp.zeros((BF16_ROWS, QB), F32))
        return jnp.sum(acc, axis=0, keepdims=True).astype(I32)

    top_key = jnp.full((1, QB), INT_MIN, I32)
    for bit in range(31, 15, -1):
        cand = jnp.zeros((1, QB), I32) if bit == 31 else top_key | np.int32(1 << bit)
        cnt = count_bf16(key_value(cand).astype(BF16))
        top_key = jnp.where(cnt >= topk, cand, top_key)
    bf16_step = np.int32(1 << 16)
    thr_key = jnp.maximum(top_key, INT_MIN + bf16_step) - bf16_step
    hi_key = top_key + (bf16_step - 1)
    for _ in range(17):
        mid = thr_key + ((hi_key - thr_key + 1) >> 1)
        ok = count(lambda s, v=key_value(mid): s >= v) >= topk
        thr_key = jnp.where(ok, mid, thr_key)
        hi_key = jnp.where(ok, hi_key, mid - 1)
    thr = jnp.where(thr_key <= KEY_NEG_INF, -jnp.inf, key_value(thr_key))
    need = (topk - count(lambda s: s > thr)).astype(F32)

    tri = tri_ref[...]

    acc_scr[...] = jnp.zeros_like(acc_scr)

    def attend(c, carry):
        run, ms = carry
        key = sc_scr[c]
        eq = key == thr
        eqf = jnp.where(eq, 1.0, 0.0)
        before = _dot(tri, eqf.astype(BF16)) + run
        sel = (key > thr) | (eq & (before < need))
        sel = sel & (c * QB + lax.broadcasted_iota(I32, (QB, QB), 0) <=
                     i * QB + lax.broadcasted_iota(I32, (QB, QB), 1))
        bias = jnp.where(sel, 0.0, -jnp.inf)
        run = run + _colsum(eqf)
        heads = range(ATTN_HEADS)
        s = _dot(ckv_ref[0, c], qabs_ref[0])
        sh = [s[:, h * QB:(h + 1) * QB] + bias for h in heads]
        new_ms = [jnp.maximum(ms[h], _colmax(sh[h])) for h in heads]
        shift = [jnp.where(m == -jnp.inf, 0.0, m) for m in new_ms]
        ps = [jnp.exp2((sh[h] - shift[h]).astype(BF16)) for h in heads]
        outs = [_dot(vt_ref[0, c, h * VROWS:(h + 1) * VROWS, :], ps[h]) for h in heads]
        scales = [jnp.broadcast_to(jnp.exp2(ms[h] - shift[h]), (VROWS, QB)) for h in heads]
        acc_scr[...] = acc_scr[...] * jnp.concatenate(scales, axis=0) + \
            jnp.concatenate(outs, axis=0)
        return run, tuple(new_ms)

    ninf = jnp.full((1, QB), -jnp.inf, F32)
    lax.fori_loop(0, i + 1, attend,
                  (jnp.zeros((1, QB), F32), tuple(ninf for _ in range(ATTN_HEADS))))

    o = jnp.concatenate(
        [acc_scr[h * VROWS:h * VROWS + HEAD_DIM, :] *
         (1.0 / acc_scr[h * VROWS + HEAD_DIM:h * VROWS + HEAD_DIM + 1, :])
         for h in range(ATTN_HEADS)], axis=0)
    o = jnp.transpose(o)
    gt = gate_ref[...]
    o_ref[...] = (o * (gt * _sigmoid(gt))).astype(o_ref.dtype)


def _dsa(p, prep, tri, b, t, topk):
    nb = t // QB
    ckv, kidx, vt, qidx, qabs, wit = prep
    nkb = t // KB
    in_specs = [pl.BlockSpec((1, nkb, KB, LANES), lambda bi, qi: (bi, 0, 0, 0)),
                pl.BlockSpec((1, nb, QB, KV_RANK), lambda bi, qi: (bi, 0, 0, 0)),
                pl.BlockSpec((1, nb, ATTN_HEADS * VROWS, QB), lambda bi, qi: (bi, 0, 0, 0)),
                pl.BlockSpec((1, LANES, IDX_HEADS * QB), lambda bi, qi: (bi, 0, qi)),
                pl.BlockSpec((1, KV_RANK, ATTN_HEADS * QB), lambda bi, qi: (bi, 0, qi)),
                pl.BlockSpec((1, SUBLANES, QB), lambda bi, qi: (bi, 0, qi)),
                pl.BlockSpec((QB, ATTN_WIDTH), lambda bi, qi: (bi * nb + qi, C_GA // ATTN_WIDTH)),
                pl.BlockSpec((QB, QB), lambda bi, qi: (0, 0))]
    return pl.pallas_call(
        functools.partial(_dsa_kernel, topk=topk),
        grid=(b, nb),
        in_specs=in_specs,
        out_specs=pl.BlockSpec((QB, ATTN_WIDTH), lambda bi, qi: (bi * nb + qi, 0)),
        out_shape=jax.ShapeDtypeStruct((b * t, ATTN_WIDTH), BF16),
        scratch_shapes=[pltpu.VMEM((nb, QB, QB), F32), pltpu.VMEM((nb, QB, QB), BF16),
                        pltpu.VMEM((ATTN_HEADS * VROWS, QB), F32)],
        compiler_params=_params(("parallel", "arbitrary")),
        name="dsa",
    )(kidx, ckv, vt, qidx, qabs, wit, p, tri)


def _outproj_kernel(yr_ref, ya_ref, x_ref, w_ref, g_ref, o_ref):
    w = w_ref[...]
    y = _dot(yr_ref[...], w[0:RWKV_WIDTH]) + _dot(ya_ref[...], w[RWKV_WIDTH:])
    z = x_ref[...] + y
    ms = jnp.mean(z * z, axis=-1, keepdims=True)
    o_ref[...] = z * lax.rsqrt(ms + NORM_EPS) * g_ref[...]


def _outproj(yr, ya, x2, w_out, final_g, tm):
    n, d = x2.shape
    return pl.pallas_call(
        _outproj_kernel,
        grid=(n // tm,),
        in_specs=[pl.BlockSpec((tm, RWKV_WIDTH), lambda i: (i, 0)),
                  pl.BlockSpec((tm, ATTN_WIDTH), lambda i: (i, 0)),
                  pl.BlockSpec((tm, d), lambda i: (i, 0)),
                  pl.BlockSpec(w_out.shape, lambda i: (0, 0)),
                  pl.BlockSpec((1, d), lambda i: (0, 0))],
        out_specs=pl.BlockSpec((tm, d), lambda i: (i, 0)),
        out_shape=jax.ShapeDtypeStruct((n, d), F32),
        compiler_params=_params(("parallel",)),
        name="outproj",
    )(yr, ya, x2, w_out, final_g)


class _BlockRows(NamedTuple):
    inproj_rows: int
    rwkv_rows: int
    outproj_rows: int
    dsaprep_rows: int


def _block_rows(b, t):
    n = b * t
    assert t % KB == 0 and t % (A_GROUP * CHUNK) == 0
    pick = lambda want, total: want if total % want == 0 else KB
    return _BlockRows(inproj_rows=pick(512, n), rwkv_rows=pick(512, t),
                      outproj_rows=pick(1024, n), dsaprep_rows=pick(1024, t))


def _pad_rows(w, start, total):
    return jnp.zeros((total, w.shape[1]), w.dtype).at[start:start + w.shape[0]].set(w)


def _layer(x2, b, t, norm_g, w_in, mu_shift, w0, w_up, a0, a_up, k_k, k_a, r_k, gn_g, gn_b,
           q_norm_g, kv_norm_g, w_uq, w_uk, w_uv, w_qidx, kidx_g, kidx_b, w_out, final_g):
    W = RWKV_WIDTH
    d = x2.shape[1]
    o_r, o_k, o_v = 0, W, 2 * W
    o_wd = 3 * W
    o_ad = o_wd + DECAY_LORA
    o_gr = o_ad + AAA_LORA
    o_qd = o_gr + W
    o_kv = o_qd + Q_RANK
    o_ki = o_kv + KV_RANK
    o_wi = o_ki + IDX_DIM
    o_ga = o_wi + IDX_HEADS
    cols = lambda s, n: w_in[:, s:s + n]
    w_all = jnp.concatenate(
        [cols(o_r, 3 * W), cols(o_gr, W), cols(o_ga, ATTN_WIDTH), cols(o_qd, Q_RANK),
         cols(o_kv, KV_RANK), cols(o_wd, DECAY_LORA), cols(o_ad, AAA_LORA), cols(o_ki, IDX_DIM),
         cols(o_wi, IDX_HEADS), jnp.zeros((d, LANES - IDX_HEADS), w_in.dtype)],
        axis=1).astype(BF16)

    row2 = lambda v: v.reshape(1, -1).astype(F32)
    mu_rkv = row2(mu_shift[0:3 * W])
    mu_sm = row2(jnp.concatenate([mu_shift[3 * W:], jnp.zeros((IDX_DIM,), F32)]))
    wup_pad = _pad_rows(w_up, 0, LANES).astype(BF16)
    aup_pad = _pad_rows(a_up, DECAY_LORA, LANES).astype(BF16)
    hid = np.arange(W // 2) // HEAD_DIM
    bd = jnp.asarray((hid[:, None] == hid[None, :]).astype(np.float32), dtype=BF16)

    blk = _block_rows(b, t)
    ti = np.arange(CHUNK)
    tri_r = jnp.asarray((ti[None, :] <= ti[:, None]).astype(np.float32), dtype=BF16)
    rwkv_consts = [mu_rkv, mu_sm, row2(w0), wup_pad, row2(a0), aup_pad, row2(k_k), row2(k_a),
                   row2(r_k), row2(gn_g), row2(gn_b), bd, tri_r]

    wq_t = jnp.transpose(w_qidx.reshape(Q_RANK, IDX_HEADS, IDX_DIM), (1, 2, 0))
    wq_t = jnp.concatenate([jnp.zeros_like(wq_t), wq_t], axis=1).astype(BF16)
    w_uq_heads = jnp.transpose(w_uq.reshape(Q_RANK, ATTN_HEADS, HEAD_DIM), (1, 0, 2))
    wabs_t = _absorb(w_uk, w_uq_heads)
    wuv_t = jnp.transpose(w_uv, (0, 2, 1)).reshape(ATTN_WIDTH, KV_RANK).astype(BF16)
    lng = row2(jnp.concatenate([jnp.zeros((IDX_DIM,), F32), kidx_g]))
    lnb = row2(jnp.concatenate([jnp.zeros((IDX_DIM,), F32), kidx_b]))
    prep_consts = [row2(q_norm_g), row2(kv_norm_g), lng, lnb, wq_t, wabs_t, wuv_t]
    ki = np.arange(QB)
    tri_q = jnp.asarray((ki[None, :] < ki[:, None]).astype(np.float32), dtype=BF16)

    p = _inproj(x2, row2(norm_g), w_all, blk.inproj_rows)
    y_r = _rwkv(p, b, t, blk.rwkv_rows, rwkv_consts)
    prep = _dsaprep(p, b, t, blk.dsaprep_rows, prep_consts)
    y_a = _dsa(p, prep, tri_q, b, t, min(TOPK_MAX, t // 4))
    return _outproj(y_r, y_a, x2, w_out.astype(BF16), row2(final_g), blk.outproj_rows)


def kernel(x, norm_g, w_in, mu_shift, w0, w_up, a0, a_up, k_k, k_a, r_k, gn_g, gn_b,
           q_norm_g, kv_norm_g, w_uq, w_uk, w_uv, w_qidx, kidx_g, kidx_b, w_out, final_g):
    b, t, d = x.shape
    assert norm_g.shape[0] == 1, "single-layer problem"
    out = _layer(x.reshape(b * t, d), b, t, norm_g[0], w_in[0], mu_shift[0], w0[0], w_up[0],
                 a0[0], a_up[0], k_k[0], k_a[0], r_k[0], gn_g[0], gn_b[0], q_norm_g[0],
                 kv_norm_g[0], w_uq[0], w_uk[0], w_uv[0], w_qidx[0], kidx_g[0], kidx_b[0],
                 w_out[0], final_g)
    return out.reshape(b, t, d)
```

```python
import functools
from typing import NamedTuple

import numpy as np
import jax
import jax.numpy as jnp
from jax import lax
from jax.experimental import pallas as pl
from jax.experimental.pallas import tpu as pltpu

F32 = jnp.float32
BF16 = jnp.bfloat16
I32 = jnp.int32

HEAD_DIM = 64
RWKV_HEADS = 8
RWKV_WIDTH = RWKV_HEADS * HEAD_DIM
DECAY_LORA = 32
AAA_LORA = 32
ATTN_HEADS = 8
ATTN_WIDTH = ATTN_HEADS * HEAD_DIM
Q_RANK = 256
KV_RANK = 128
IDX_HEADS = 4
IDX_DIM = 64
TOPK_MAX = 256
NORM_EPS = 1e-6
LN_EPS = 1e-5
GN_EPS = 64e-5

LANES = 128
SUBLANES = 8
BF16_ROWS = 16
QB = 256
KB = 2 * QB
CHUNK = 64
PAIR = 2 * HEAD_DIM
A_GROUP = 2
VROWS = HEAD_DIM + BF16_ROWS

C_RKV = 0
C_GR = 3 * RWKV_WIDTH
C_GA = C_GR + RWKV_WIDTH
C_QD = C_GA + ATTN_WIDTH
C_KV = C_QD + Q_RANK
C_SM = C_KV + KV_RANK
C_WI = C_SM + LANES
P_COLS = C_WI + LANES

VMEM_LIMIT = 56 * 1024 * 1024


def _params(sem):
    return pltpu.CompilerParams(dimension_semantics=sem, vmem_limit_bytes=VMEM_LIMIT)


def _dot(a, b):
    return jnp.dot(a, b, preferred_element_type=F32)


def _dot_nt(a, b):
    return lax.dot_general(a, b, (((1,), (1,)), ((), ())), preferred_element_type=F32)


def _dot_tn(a, b):
    return lax.dot_general(a, b, (((0,), (0,)), ((), ())), preferred_element_type=F32)


def _split2(x):
    hi = x.astype(BF16)
    lo = (x - hi.astype(F32)).astype(BF16)
    return hi, lo


def _split3(x):
    hi = x.astype(BF16)
    r1 = x - hi.astype(F32)
    mid = r1.astype(BF16)
    lo = (r1 - mid.astype(F32)).astype(BF16)
    return hi, mid, lo


def _fold8(x, op, rows=SUBLANES):
    acc = x[0:rows]
    for j in range(1, x.shape[0] // rows):
        acc = op(acc, x[j * rows:(j + 1) * rows])
    return acc


def _colsum(x):
    return jnp.sum(_fold8(x, jnp.add), axis=0, keepdims=True)


def _colmax(x, rows=SUBLANES):
    return jnp.max(_fold8(x, jnp.maximum, rows), axis=0, keepdims=True)


def _sigmoid(x):
    return 1.0 / (1.0 + jnp.exp(-x))


def _absorb_kernel(wuk_ref, wuq_ref, o_ref):
    a = wuk_ref[0]
    b = wuq_ref[0]
    ah, al = _split2(a)
    bh, bl = _split2(b)
    acc = _dot_nt(ah, bh) + _dot_nt(ah, bl) + _dot_nt(al, bh)
    o_ref[0] = acc * np.float32(HEAD_DIM ** -0.5 * np.log2(np.e))


def _absorb(w_uk, w_uq_heads):
    return pl.pallas_call(
        _absorb_kernel,
        grid=(ATTN_HEADS,),
        in_specs=[pl.BlockSpec((1, KV_RANK, HEAD_DIM), lambda h: (h, 0, 0)),
                  pl.BlockSpec((1, Q_RANK, HEAD_DIM), lambda h: (h, 0, 0))],
        out_specs=pl.BlockSpec((1, KV_RANK, Q_RANK), lambda h: (h, 0, 0)),
        out_shape=jax.ShapeDtypeStruct((ATTN_HEADS, KV_RANK, Q_RANK), F32),
        compiler_params=_params(("arbitrary",)),
        name="absorb",
    )(w_uk, w_uq_heads)


def _inproj_kernel(x_ref, g_ref, w_ref, o_ref):
    x = x_ref[...]
    ms = jnp.mean(x * x, axis=-1, keepdims=True)
    xn = x * lax.rsqrt(ms + NORM_EPS) * g_ref[...]
    o_ref[...] = _dot(xn.astype(BF16), w_ref[...])


def _inproj(x2, norm_g, w_all, tm):
    n, d = x2.shape
    return pl.pallas_call(
        _inproj_kernel,
        grid=(n // tm,),
        in_specs=[pl.BlockSpec((tm, d), lambda i: (i, 0)),
                  pl.BlockSpec((1, d), lambda i: (0, 0)),
                  pl.BlockSpec((d, P_COLS), lambda i: (0, 0))],
        out_specs=pl.BlockSpec((tm, P_COLS), lambda i: (i, 0)),
        out_shape=jax.ShapeDtypeStruct((n, P_COLS), F32),
        compiler_params=_params(("parallel",)),
        name="inproj",
    )(x2, norm_g, w_all)


def _rwkv_kernel(rkv_ref, sm_ref, gate_ref, mu_rkv_ref, mu_sm_ref, w0_ref, wup_ref, a0_ref,
                 aup_ref, kkw_ref, ka_ref, rk_ref, gng_ref, gnb_ref, bd_ref, tri_ref,
                 o_ref,
                 carry_rkv, carry_sm, s_scr, n_scr, r_scr, u_scr, k_scr, v_scr, wc_scr, bv_scr,
                 t_scr, ankv_scr, ar_scr, *, tb):
    t = pl.program_id(1)
    W = RWKV_WIDTH

    @pl.when(t == 0)
    def _():
        carry_rkv[...] = jnp.zeros_like(carry_rkv)
        carry_sm[...] = jnp.zeros_like(carry_sm)
        s_scr[...] = jnp.zeros_like(s_scr)

    bd = bd_ref[...]
    half = W // 2

    def head_sum(x):
        st = jnp.concatenate([x[:, :half], x[:, half:]], axis=0)
        res = _dot(st.astype(BF16), bd)
        return jnp.concatenate([res[:CHUNK], res[CHUNK:]], axis=1)

    def shifted(ref, carry, c, rows):
        cur = ref[rows, :]
        first = carry[0:1, :] if c == 0 else ref[c * CHUNK - 1:c * CHUNK, :]
        row = lax.broadcasted_iota(I32, (CHUNK, 1), 0)
        return cur, jnp.where(row == 0, first, pltpu.roll(cur, 1, 0))

    def prep(c):
        rows = pl.ds(c * CHUNK, CHUNK)
        p, prev = shifted(rkv_ref, carry_rkv, c, rows)
        xs = p + (prev - p) * mu_rkv_ref[...]
        r = xs[:, 0:W]
        k = xs[:, W:2 * W]
        v = xs[:, 2 * W:3 * W]
        ps, prev_s = shifted(sm_ref, carry_sm, c, rows)
        xl = ps + (prev_s - ps) * mu_sm_ref[...]
        zw = w0_ref[...] + _dot(jnp.tanh(xl).astype(BF16), wup_ref[...])
        za = a0_ref[...] + _dot(xl.astype(BF16), aup_ref[...])
        yield
        logw = -np.float32(np.exp(-0.5)) * _sigmoid(zw)
        a = _sigmoid(za)
        kk = k * kkw_ref[...]
        ss = head_sum(kk * kk)
        hi, mid, lo = _split3(logw)
        tri = tri_ref[...]
        cum = _dot(tri, hi) + _dot(tri, mid) + _dot(tri, lo)
        yield
        kk = kk * (1.0 / jnp.maximum(jnp.sqrt(ss), 1e-12))
        k2 = k * (1.0 + (a - 1.0) * ka_ref[...])
        bonus = head_sum(r * k2 * rk_ref[...])
        wt = jnp.exp(cum)
        wi = jnp.exp(-cum)
        wx = jnp.exp(cum - logw)
        n_scr[rows, :] = (-kk * wx).astype(BF16)
        r_scr[rows, :] = (r * wt).astype(BF16)
        u_scr[rows, :] = (kk * a * wi).astype(BF16)
        k_scr[rows, :] = (k2 * wi).astype(BF16)
        v_scr[rows, :] = v.astype(BF16)
        wc_scr[c] = wt[CHUNK - SUBLANES:CHUNK, :]
        yield
        bv_scr[rows, :] = bonus * v

    rowp = lax.broadcasted_iota(I32, (PAIR, PAIR), 0)
    colp = lax.broadcasted_iota(I32, (PAIR, PAIR), 1)
    same_blk = (rowp >= HEAD_DIM) == (colp >= HEAD_DIM)
    strict_lo = (colp % HEAD_DIM) < (rowp % HEAD_DIM)
    m_abd = same_blk & strict_lo
    m_ank = jnp.logical_not(same_blk) & strict_lo
    rowc = lax.broadcasted_iota(I32, (CHUNK, PAIR), 0)
    colc = lax.broadcasted_iota(I32, (CHUNK, PAIR), 1)
    incl_lo2 = (lax.broadcasted_iota(I32, (CHUNK, 2 * PAIR), 1) % HEAD_DIM) <= \
        lax.broadcasted_iota(I32, (CHUNK, 2 * PAIR), 0)
    head0_c = colc < HEAD_DIM
    head0_p = colp < HEAD_DIM

    pairs = range(RWKV_HEADS // 2)
    lanes = [slice(g * PAIR, (g + 1) * PAIR) for g in pairs]
    eye = (rowp == colp).astype(F32)

    def phase_a(chunks):
        chains = [(c, g) for c in chunks for g in pairs]
        gtop = []
        for c, g in chains:
            rows = pl.ds(c * CHUNK, CHUNK)
            sl = lanes[g]
            nr = jnp.concatenate([n_scr[rows, sl], r_scr[rows, sl]], axis=0)
            uu = u_scr[rows, sl]
            kc = k_scr[rows, sl]
            zero = jnp.zeros_like(nr)
            rhs = jnp.concatenate(
                [jnp.where(head0_p, jnp.concatenate([uu, kc], axis=0), zero),
                 jnp.where(head0_p, zero, jnp.concatenate([kc, uu], axis=0))], axis=0)
            gg = _dot_nt(nr, rhs)
            ar_scr[c, g] = jnp.where(incl_lo2, gg[CHUNK:PAIR], 0.0).astype(BF16)
            gtop.append(jnp.concatenate([gg[0:CHUNK, 0:PAIR], gg[0:CHUNK, PAIR:]], axis=0))
        yield
        for i, (c, g) in enumerate(chains):
            vv = v_scr[pl.ds(c * CHUNK, CHUNK), lanes[g]]
            ank = jnp.where(m_ank, gtop[i], 0.0).astype(BF16)
            ankv_scr[c, g] = _dot(ank, jnp.concatenate([vv, vv], axis=0))
        n = range(len(chains))
        pw = [jnp.where(m_abd, gtop[i], 0.0) for i in n]
        tm = [eye + pw[i] for i in n]
        pwb = [x.astype(BF16) for x in pw]
        pw = [_dot(x, x) for x in pwb]
        yield
        for _ in range(4):
            pwb = [x.astype(BF16) for x in pw]
            both = [_dot(pwb[i], jnp.concatenate([pwb[i], tm[i].astype(BF16)], axis=1))
                    for i in n]
            pw = [x[:, 0:PAIR] for x in both]
            tm = [tm[i] + both[i][:, PAIR:] for i in n]
            yield
        tm = [tm[i] + _dot(pw[i].astype(BF16), tm[i].astype(BF16)) for i in n]
        for i, (c, g) in enumerate(chains):
            t_scr[c, g] = tm[i].astype(BF16)

    def phase_b(c):
        rows = pl.ds(c * CHUNK, CHUNK)
        wc_all = wc_scr[c][SUBLANES - 1:SUBLANES, :]
        s_old = [s_scr[g] for g in pairs]
        nrh = [_dot_nt(jnp.concatenate([n_scr[rows, lanes[g]], r_scr[rows, lanes[g]]], axis=0),
                       s_old[g].astype(BF16)) for g in pairs]
        yield
        zf = []
        for g in pairs:
            nh = nrh[g][0:CHUNK]
            bf = jnp.concatenate([nh, nh], axis=0) + ankv_scr[c, g]
            zf.append(_dot(t_scr[c, g], bf.astype(BF16)))
        yield
        ys = []
        for g in pairs:
            sl = lanes[g]
            vv = v_scr[rows, sl]
            zb = jnp.where(head0_c, zf[g][0:CHUNK], zf[g][CHUNK:PAIR]).astype(BF16)
            zv = jnp.concatenate([zb, vv], axis=0)
            vz = jnp.concatenate([vv, zb], axis=0)
            zero = jnp.zeros_like(zv)
            rhs = jnp.concatenate([jnp.where(head0_p, zv, zero), jnp.where(head0_p, zero, vz)],
                                  axis=0)
            ys.append(nrh[g][CHUNK:PAIR] + _dot(ar_scr[c, g], rhs))
            uk = jnp.concatenate([u_scr[rows, sl], k_scr[rows, sl]], axis=0)
            s_new = (s_old[g] + _dot_tn(zv, uk)) * wc_all[:, sl]
            s_scr[g] = jnp.where(same_blk, s_new, 0.0)
        yield
        y = jnp.concatenate(ys, axis=1)
        inv_d = np.float32(1.0 / HEAD_DIM)
        yc = y - head_sum(y) * inv_d
        yield
        var = head_sum(yc * yc) * inv_d
        y = yc * lax.rsqrt(var + GN_EPS) * gng_ref[...] + gnb_ref[...] + bv_scr[rows, :]
        gt = gate_ref[rows, :]
        o_ref[rows, :] = (y * (gt * _sigmoid(gt))).astype(o_ref.dtype)

    def in_step(gens):
        gens = list(gens)
        while gens:
            gens = [g for g in gens if next(g, True) is None]
            if gens:
                yield

    def in_turn(gens):
        for g in gens:
            yield from g

    nchunks = tb // CHUNK
    groups = [list(range(c0, c0 + A_GROUP)) for c0 in range(0, nchunks, A_GROUP)]
    for s in range(len(groups) + 2):
        streams = []
        if s - 2 >= 0:
            streams.append(in_turn(phase_b(c) for c in groups[s - 2]))
        if 0 <= s - 1 < len(groups):
            streams.append(phase_a(groups[s - 1]))
        if s < len(groups):
            streams.append(in_step(prep(c) for c in groups[s]))
        while streams:
            streams = [g for g in streams if next(g, True) is None]
    carry_rkv[0:1, :] = rkv_ref[tb - 1:tb, :]
    carry_sm[0:1, :] = sm_ref[tb - 1:tb, :]


def _rwkv(p, b, t, tb, consts):
    nt = t // tb
    W = RWKV_WIDTH
    row_spec = lambda width, cb: pl.BlockSpec((tb, width), lambda bi, ti: (bi * nt + ti, cb))
    full = lambda arr: pl.BlockSpec(arr.shape, lambda bi, ti: (0,) * arr.ndim)
    in_specs = [row_spec(3 * W, C_RKV // (3 * W)), row_spec(LANES, C_SM // LANES),
                row_spec(W, C_GR // W)] + [full(c) for c in consts]
    return pl.pallas_call(
        functools.partial(_rwkv_kernel, tb=tb),
        grid=(b, nt),
        in_specs=in_specs,
        out_specs=pl.BlockSpec((tb, W), lambda bi, ti: (bi * nt + ti, 0)),
        out_shape=jax.ShapeDtypeStruct((b * t, W), BF16),
        scratch_shapes=[pltpu.VMEM((SUBLANES, 3 * W), F32), pltpu.VMEM((SUBLANES, LANES), F32),
                        pltpu.VMEM((RWKV_HEADS // 2, PAIR, PAIR), F32),
                        pltpu.VMEM((tb, W), BF16), pltpu.VMEM((tb, W), BF16),
                        pltpu.VMEM((tb, W), BF16), pltpu.VMEM((tb, W), BF16),
                        pltpu.VMEM((tb, W), BF16),
                        pltpu.VMEM((tb // CHUNK, SUBLANES, W), F32),
                        pltpu.VMEM((tb, W), F32),
                        pltpu.VMEM((tb // CHUNK, RWKV_HEADS // 2, PAIR, PAIR), BF16),
                        pltpu.VMEM((tb // CHUNK, RWKV_HEADS // 2, PAIR, PAIR), F32),
                        pltpu.VMEM((tb // CHUNK, RWKV_HEADS // 2, CHUNK, 2 * PAIR), BF16)],
        compiler_params=_params(("parallel", "arbitrary")),
        name="rwkv",
    )(p, p, p, *consts)


def _dsaprep_kernel(qd_ref, kv_ref, sm_ref, wi_ref, qg_ref, kvg_ref, lng_ref, lnb_ref,
                    wqidx_ref, wabs_ref, wuv_ref,
                    ckv_ref, kidx_ref, vt_ref, qidx_ref, qabs_ref, wit_ref, *, tb):
    nq = tb // QB
    qd = qd_ref[...]
    cq = qd * lax.rsqrt(jnp.mean(qd * qd, axis=-1, keepdims=True) + NORM_EPS) * qg_ref[...]
    cqb = cq.astype(BF16)
    kv = kv_ref[...]
    ckv = kv * lax.rsqrt(jnp.mean(kv * kv, axis=-1, keepdims=True) + NORM_EPS) * kvg_ref[...]
    ckvb = ckv.astype(BF16)

    sm = sm_ref[...]
    lane = lax.broadcasted_iota(I32, sm.shape, 1)
    hi_half = lane >= IDX_DIM
    inv_d = np.float32(1.0 / IDX_DIM)
    mu = jnp.sum(jnp.where(hi_half, sm, 0.0), axis=-1, keepdims=True) * inv_d
    xc = jnp.where(hi_half, sm - mu, 0.0)
    var = jnp.sum(xc * xc, axis=-1, keepdims=True) * inv_d
    kidx = (xc * lax.rsqrt(var + LN_EPS) * lng_ref[...] + lnb_ref[...]).astype(BF16)

    vt = _dot_nt(wuv_ref[...], ckvb).astype(BF16)
    ones = jnp.ones((VROWS - HEAD_DIM, tb), BF16)
    vt = jnp.concatenate(
        [x for h in range(ATTN_HEADS) for x in (vt[h * HEAD_DIM:(h + 1) * HEAD_DIM], ones)],
        axis=0)
    wit = jnp.transpose(wi_ref[...])[0:SUBLANES, :] * np.float32(IDX_HEADS ** -0.5)
    for j in range(tb // KB):
        kidx_ref[0, j] = kidx[j * KB:(j + 1) * KB, :]
    for j in range(nq):
        ckv_ref[0, j] = ckvb[j * QB:(j + 1) * QB, :]
        vt_ref[0, j] = vt[:, j * QB:(j + 1) * QB]
    wit_ref[0] = wit
    for h in range(IDX_HEADS):
        res = _dot_nt(wqidx_ref[h], cqb).astype(BF16)
        for j in range(nq):
            qidx_ref[0, :, (j * IDX_HEADS + h) * QB:(j * IDX_HEADS + h + 1) * QB] = \
                res[:, j * QB:(j + 1) * QB]
    for h in range(ATTN_HEADS):
        res = _dot_nt(wabs_ref[h].astype(BF16), cqb).astype(BF16)
        for j in range(nq):
            qabs_ref[0, :, (j * ATTN_HEADS + h) * QB:(j * ATTN_HEADS + h + 1) * QB] = \
                res[:, j * QB:(j + 1) * QB]


def _dsaprep(p, b, t, tb, consts):
    nt = t // tb
    nb = t // QB
    nq = tb // QB
    row_spec = lambda width, cb: pl.BlockSpec((tb, width), lambda bi, ti: (bi * nt + ti, cb))
    full = lambda arr: pl.BlockSpec(arr.shape, lambda bi, ti: (0,) * arr.ndim)
    in_specs = [row_spec(Q_RANK, C_QD // Q_RANK), row_spec(KV_RANK, C_KV // KV_RANK),
                row_spec(LANES, C_SM // LANES), row_spec(LANES, C_WI // LANES)] + \
               [full(c) for c in consts]
    nkb = t // KB
    nk = tb // KB
    out_shape = [jax.ShapeDtypeStruct((b, nb, QB, KV_RANK), BF16),
                 jax.ShapeDtypeStruct((b, nkb, KB, LANES), BF16),
                 jax.ShapeDtypeStruct((b, nb, ATTN_HEADS * VROWS, QB), BF16),
                 jax.ShapeDtypeStruct((b, LANES, nb * IDX_HEADS * QB), BF16),
                 jax.ShapeDtypeStruct((b, KV_RANK, nb * ATTN_HEADS * QB), BF16),
                 jax.ShapeDtypeStruct((b, SUBLANES, t), F32)]
    out_specs = [pl.BlockSpec((1, nq, QB, KV_RANK), lambda bi, ti: (bi, ti, 0, 0)),
                 pl.BlockSpec((1, nk, KB, LANES), lambda bi, ti: (bi, ti, 0, 0)),
                 pl.BlockSpec((1, nq, ATTN_HEADS * VROWS, QB), lambda bi, ti: (bi, ti, 0, 0)),
                 pl.BlockSpec((1, LANES, nq * IDX_HEADS * QB), lambda bi, ti: (bi, 0, ti)),
                 pl.BlockSpec((1, KV_RANK, nq * ATTN_HEADS * QB), lambda bi, ti: (bi, 0, ti)),
                 pl.BlockSpec((1, SUBLANES, tb), lambda bi, ti: (bi, 0, ti))]
    return pl.pallas_call(
        functools.partial(_dsaprep_kernel, tb=tb),
        grid=(b, nt),
        in_specs=in_specs,
        out_specs=out_specs,
        out_shape=out_shape,
        compiler_params=_params(("parallel", "parallel")),
        name="dsaprep",
    )(p, p, p, p, *consts)


INT_MIN = np.int32(-2 ** 31)
KEY_NEG_INF = np.int32(np.array(0xFF800000, np.uint32).view(np.int32) ^ np.int32(0x7FFFFFFF))


def _dsa_kernel(kidx_ref, ckv_ref, vt_ref, qidx_ref, qabs_ref, wi_ref, gate_ref, tri_ref,
                o_ref, sc_scr, scb_scr, acc_scr, *, topk):
    i = pl.program_id(1)
    nkb = i // (KB // QB) + 1
    row = lax.broadcasted_iota(I32, (KB, QB), 0)
    col = lax.broadcasted_iota(I32, (KB, QB), 1)
    qpos = i * QB + col
    wi = wi_ref[0]

    def score_body(j, carry):
        lg = _dot(kidx_ref[0, j], qidx_ref[0])
        sc = jnp.zeros((KB, QB), F32)
        for h in range(IDX_HEADS):
            sc = sc + wi[h:h + 1, :] * jnp.maximum(lg[:, h * QB:(h + 1) * QB], 0.0)
        sc = jnp.where(j * KB + row <= qpos, sc, -jnp.inf)
        sc_scr[2 * j] = sc[0:QB]
        sc_scr[2 * j + 1] = sc[QB:KB]
        scb = sc.astype(BF16)
        scb_scr[2 * j] = scb[0:QB]
        scb_scr[2 * j + 1] = scb[QB:KB]
        return carry

    lax.fori_loop(0, nkb, score_body, 0)

    def count(pred):
        def body(j, accs):
            a0, a1 = accs
            a0 = a0 + _fold8(jnp.where(pred(sc_scr[2 * j]), 1, 0), jnp.add)
            a1 = a1 + _fold8(jnp.where(pred(sc_scr[2 * j + 1]), 1, 0), jnp.add)
            return a0, a1
        zero = jnp.zeros((SUBLANES, QB), I32)
        a0, a1 = lax.fori_loop(0, nkb, body, (zero, zero))
        return jnp.sum(a0 + a1, axis=0, keepdims=True)

    def key_value(key):
        return lax.bitcast_convert_type(jnp.where(key < 0, key ^ np.int32(0x7FFFFFFF), key), F32)

    def count_bf16(cand):
        one = jnp.ones((QB, QB), BF16)
        zero = jnp.zeros((QB, QB), BF16)

        def body(j, acc):
            for c in (2 * j, 2 * j + 1):
                hits = jnp.where(scb_scr[c] >= cand, one, zero)
                acc = acc + _fold8(hits, jnp.add, rows=BF16_ROWS).astype(F32)
            return acc
        acc = lax.fori_loop(0, nkb, body, jnp.zeros((BF16_ROWS, QB), F32))
        return jnp.sum(acc, axis=0, keepdims=True).astype(I32)

    top_key = jnp.full((1, QB), INT_MIN, I32)
    for bit in range(31, 15, -1):
        cand = jnp.zeros((1, QB), I32) if bit == 31 else top_key | np.int32(1 << bit)
        cnt = count_bf16(key_value(cand).astype(BF16))
        top_key = jnp.where(cnt >= topk, cand, top_key)
    bf16_step = np.int32(1 << 16)
    thr_key = jnp.maximum(top_key, INT_MIN + bf16_step) - bf16_step
    hi_key = top_key + (bf16_step - 1)
    for _ in range(17):
        mid = thr_key + ((hi_key - thr_key + 1) >> 1)
        ok = count(lambda s, v=key_value(mid): s >= v) >= topk
        thr_key = jnp.where(ok, mid, thr_key)
        hi_key = jnp.where(ok, hi_key, mid - 1)
    thr = jnp.where(thr_key <= KEY_NEG_INF, -jnp.inf, key_value(thr_key))
    need = (topk - count(lambda s: s > thr)).astype(F32)

    tri = tri_ref[...]

    acc_scr[...] = jnp.zeros_like(acc_scr)

    def attend(c, carry):
        run, ms = carry
        key = sc_scr[c]
        eq = key == thr
        eqf = jnp.where(eq, 1.0, 0.0)
        before = _dot(tri, eqf.astype(BF16)) + run
        sel = (key > thr) | (eq & (before < need))
        sel = sel & (c * QB + lax.broadcasted_iota(I32, (QB, QB), 0) <=
                     i * QB + lax.broadcasted_iota(I32, (QB, QB), 1))
        bias = jnp.where(sel, 0.0, -jnp.inf)
        run = run + _colsum(eqf)
        heads = range(ATTN_HEADS)
        s = _dot(ckv_ref[0, c], qabs_ref[0])
        bias_b = bias.astype(BF16)
        sh = [s[:, h * QB:(h + 1) * QB].astype(BF16) + bias_b for h in heads]
        new_ms = [jnp.maximum(ms[h], _colmax(sh[h], rows=BF16_ROWS).astype(F32)) for h in heads]
        shift = [jnp.where(m == -jnp.inf, 0.0, m) for m in new_ms]
        ps = [jnp.exp2(sh[h] - shift[h].astype(BF16)) for h in heads]
        outs = [_dot(vt_ref[0, c, h * VROWS:(h + 1) * VROWS, :], ps[h]) for h in heads]
        scales = [jnp.broadcast_to(jnp.exp2(ms[h] - shift[h]), (VROWS, QB)) for h in heads]
        acc_scr[...] = acc_scr[...] * jnp.concatenate(scales, axis=0) + \
            jnp.concatenate(outs, axis=0)
        return run, tuple(new_ms)

    ninf = jnp.full((1, QB), -jnp.inf, F32)
    lax.fori_loop(0, i + 1, attend,
                  (jnp.zeros((1, QB), F32), tuple(ninf for _ in range(ATTN_HEADS))))

    o = jnp.concatenate(
        [acc_scr[h * VROWS:h * VROWS + HEAD_DIM, :] *
         (1.0 / acc_scr[h * VROWS + HEAD_DIM:h * VROWS + HEAD_DIM + 1, :])
         for h in range(ATTN_HEADS)], axis=0)
    o = jnp.transpose(o)
    gt = gate_ref[...]
    o_ref[...] = (o * (gt * _sigmoid(gt))).astype(o_ref.dtype)


def _dsa(p, prep, tri, b, t, topk):
    nb = t // QB
    ckv, kidx, vt, qidx, qabs, wit = prep
    nkb = t // KB
    in_specs = [pl.BlockSpec((1, nkb, KB, LANES), lambda bi, qi: (bi, 0, 0, 0)),
                pl.BlockSpec((1, nb, QB, KV_RANK), lambda bi, qi: (bi, 0, 0, 0)),
                pl.BlockSpec((1, nb, ATTN_HEADS * VROWS, QB), lambda bi, qi: (bi, 0, 0, 0)),
                pl.BlockSpec((1, LANES, IDX_HEADS * QB), lambda bi, qi: (bi, 0, qi)),
                pl.BlockSpec((1, KV_RANK, ATTN_HEADS * QB), lambda bi, qi: (bi, 0, qi)),
                pl.BlockSpec((1, SUBLANES, QB), lambda bi, qi: (bi, 0, qi)),
                pl.BlockSpec((QB, ATTN_WIDTH), lambda bi, qi: (bi * nb + qi, C_GA // ATTN_WIDTH)),
                pl.BlockSpec((QB, QB), lambda bi, qi: (0, 0))]
    return pl.pallas_call(
        functools.partial(_dsa_kernel, topk=topk),
        grid=(b, nb),
        in_specs=in_specs,
        out_specs=pl.BlockSpec((QB, ATTN_WIDTH), lambda bi, qi: (bi * nb + qi, 0)),
        out_shape=jax.ShapeDtypeStruct((b * t, ATTN_WIDTH), BF16),
        scratch_shapes=[pltpu.VMEM((nb, QB, QB), F32), pltpu.VMEM((nb, QB, QB), BF16),
                        pltpu.VMEM((ATTN_HEADS * VROWS, QB), F32)],
        compiler_params=_params(("parallel", "arbitrary")),
        name="dsa",
    )(kidx, ckv, vt, qidx, qabs, wit, p, tri)


def _outproj_kernel(yr_ref, ya_ref, x_ref, w_ref, g_ref, o_ref):
    w = w_ref[...]
    y = _dot(yr_ref[...], w[0:RWKV_WIDTH]) + _dot(ya_ref[...], w[RWKV_WIDTH:])
    z = x_ref[...] + y
    ms = jnp.mean(z * z, axis=-1, keepdims=True)
    o_ref[...] = z * lax.rsqrt(ms + NORM_EPS) * g_ref[...]


def _outproj(yr, ya, x2, w_out, final_g, tm):
    n, d = x2.shape
    return pl.pallas_call(
        _outproj_kernel,
        grid=(n // tm,),
        in_specs=[pl.BlockSpec((tm, RWKV_WIDTH), lambda i: (i, 0)),
                  pl.BlockSpec((tm, ATTN_WIDTH), lambda i: (i, 0)),
                  pl.BlockSpec((tm, d), lambda i: (i, 0)),
                  pl.BlockSpec(w_out.shape, lambda i: (0, 0)),
                  pl.BlockSpec((1, d), lambda i: (0, 0))],
        out_specs=pl.BlockSpec((tm, d), lambda i: (i, 0)),
        out_shape=jax.ShapeDtypeStruct((n, d), F32),
        compiler_params=_params(("parallel",)),
        name="outproj",
    )(yr, ya, x2, w_out, final_g)


class _BlockRows(NamedTuple):
    inproj_rows: int
    rwkv_rows: int
    outproj_rows: int
    dsaprep_rows: int


def _block_rows(b, t):
    n = b * t
    assert t % KB == 0 and t % (A_GROUP * CHUNK) == 0
    pick = lambda want, total: want if total % want == 0 else KB
    return _BlockRows(inproj_rows=pick(512, n), rwkv_rows=pick(512, t),
                      outproj_rows=pick(1024, n), dsaprep_rows=pick(1024, t))


def _pad_rows(w, start, total):
    return jnp.zeros((total, w.shape[1]), w.dtype).at[start:start + w.shape[0]].set(w)


def _layer(x2, b, t, norm_g, w_in, mu_shift, w0, w_up, a0, a_up, k_k, k_a, r_k, gn_g, gn_b,
           q_norm_g, kv_norm_g, w_uq, w_uk, w_uv, w_qidx, kidx_g, kidx_b, w_out, final_g):
    W = RWKV_WIDTH
    d = x2.shape[1]
    o_r, o_k, o_v = 0, W, 2 * W
    o_wd = 3 * W
    o_ad = o_wd + DECAY_LORA
    o_gr = o_ad + AAA_LORA
    o_qd = o_gr + W
    o_kv = o_qd + Q_RANK
    o_ki = o_kv + KV_RANK
    o_wi = o_ki + IDX_DIM
    o_ga = o_wi + IDX_HEADS
    cols = lambda s, n: w_in[:, s:s + n]
    w_all = jnp.concatenate(
        [cols(o_r, 3 * W), cols(o_gr, W), cols(o_ga, ATTN_WIDTH), cols(o_qd, Q_RANK),
         cols(o_kv, KV_RANK), cols(o_wd, DECAY_LORA), cols(o_ad, AAA_LORA), cols(o_ki, IDX_DIM),
         cols(o_wi, IDX_HEADS), jnp.zeros((d, LANES - IDX_HEADS), w_in.dtype)],
        axis=1).astype(BF16)

    row2 = lambda v: v.reshape(1, -1).astype(F32)
    mu_rkv = row2(mu_shift[0:3 * W])
    mu_sm = row2(jnp.concatenate([mu_shift[3 * W:], jnp.zeros((IDX_DIM,), F32)]))
    wup_pad = _pad_rows(w_up, 0, LANES).astype(BF16)
    aup_pad = _pad_rows(a_up, DECAY_LORA, LANES).astype(BF16)
    hid = np.arange(W // 2) // HEAD_DIM
    bd = jnp.asarray((hid[:, None] == hid[None, :]).astype(np.float32), dtype=BF16)

    blk = _block_rows(b, t)
    ti = np.arange(CHUNK)
    tri_r = jnp.asarray((ti[None, :] <= ti[:, None]).astype(np.float32), dtype=BF16)
    rwkv_consts = [mu_rkv, mu_sm, row2(w0), wup_pad, row2(a0), aup_pad, row2(k_k), row2(k_a),
                   row2(r_k), row2(gn_g), row2(gn_b), bd, tri_r]

    wq_t = jnp.transpose(w_qidx.reshape(Q_RANK, IDX_HEADS, IDX_DIM), (1, 2, 0))
    wq_t = jnp.concatenate([jnp.zeros_like(wq_t), wq_t], axis=1).astype(BF16)
    w_uq_heads = jnp.transpose(w_uq.reshape(Q_RANK, ATTN_HEADS, HEAD_DIM), (1, 0, 2))
    wabs_t = _absorb(w_uk, w_uq_heads)
    wuv_t = jnp.transpose(w_uv, (0, 2, 1)).reshape(ATTN_WIDTH, KV_RANK).astype(BF16)
    lng = row2(jnp.concatenate([jnp.zeros((IDX_DIM,), F32), kidx_g]))
    lnb = row2(jnp.concatenate([jnp.zeros((IDX_DIM,), F32), kidx_b]))
    prep_consts = [row2(q_norm_g), row2(kv_norm_g), lng, lnb, wq_t, wabs_t, wuv_t]
    ki = np.arange(QB)
    tri_q = jnp.asarray((ki[None, :] < ki[:, None]).astype(np.float32), dtype=BF16)

    p = _inproj(x2, row2(norm_g), w_all, blk.inproj_rows)
    y_r = _rwkv(p, b, t, blk.rwkv_rows, rwkv_consts)
    prep = _dsaprep(p, b, t, blk.dsaprep_rows, prep_consts)
    y_a = _dsa(p, prep, tri_q, b, t, min(TOPK_MAX, t // 4))
    return _outproj(y_r, y_a, x2, w_out.astype(BF16), row2(final_g), blk.outproj_rows)


def kernel(x, norm_g, w_in, mu_shift, w0, w_up, a0, a_up, k_k, k_a, r_k, gn_g, gn_b,
           q_norm_g, kv_norm_g, w_uq, w_uk, w_uv, w_qidx, kidx_g, kidx_b, w_out, final_g):
    b, t, d = x.shape
    assert norm_g.shape[0] == 1, "single-layer problem"
    out = _layer(x.reshape(b * t, d), b, t, norm_g[0], w_in[0], mu_shift[0], w0[0], w_up[0],
                 a0[0], a_up[0], k_k[0], k_a[0], r_k[0], gn_g[0], gn_b[0], q_norm_g[0],
                 kv_norm_g[0], w_uq[0], w_uk[0], w_uv[0], w_qidx[0], kidx_g[0], kidx_b[0],
                 w_out[0], final_g)
    return out.reshape(b, t, d)
```

```python
import functools
from typing import NamedTuple

import numpy as np
import jax
import jax.numpy as jnp
from jax import lax
from jax.experimental import pallas as pl
from jax.experimental.pallas import tpu as pltpu

F32 = jnp.float32
BF16 = jnp.bfloat16
I32 = jnp.int32

HEAD_DIM = 64
RWKV_HEADS = 8
RWKV_WIDTH = RWKV_HEADS * HEAD_DIM
DECAY_LORA = 32
AAA_LORA = 32
ATTN_HEADS = 8
ATTN_WIDTH = ATTN_HEADS * HEAD_DIM
Q_RANK = 256
KV_RANK = 128
IDX_HEADS = 4
IDX_DIM = 64
TOPK_MAX = 256
NORM_EPS = 1e-6
LN_EPS = 1e-5
GN_EPS = 64e-5

LANES = 128
SUBLANES = 8
BF16_ROWS = 16
QB = 256
KB = 2 * QB
CHUNK = 64
PAIR = 2 * HEAD_DIM
A_GROUP = 2
VROWS = HEAD_DIM + BF16_ROWS

C_RKV = 0
C_GR = 3 * RWKV_WIDTH
C_GA = C_GR + RWKV_WIDTH
C_QD = C_GA + ATTN_WIDTH
C_KV = C_QD + Q_RANK
C_SM = C_KV + KV_RANK
C_WI = C_SM + LANES
P_COLS = C_WI + LANES

VMEM_LIMIT = 56 * 1024 * 1024


def _params(sem):
    return pltpu.CompilerParams(dimension_semantics=sem, vmem_limit_bytes=VMEM_LIMIT)


def _dot(a, b):
    return jnp.dot(a, b, preferred_element_type=F32)


def _dot_nt(a, b):
    return lax.dot_general(a, b, (((1,), (1,)), ((), ())), preferred_element_type=F32)


def _dot_tn(a, b):
    return lax.dot_general(a, b, (((0,), (0,)), ((), ())), preferred_element_type=F32)


def _split2(x):
    hi = x.astype(BF16)
    lo = (x - hi.astype(F32)).astype(BF16)
    return hi, lo


def _fold8(x, op, rows=SUBLANES):
    acc = x[0:rows]
    for j in range(1, x.shape[0] // rows):
        acc = op(acc, x[j * rows:(j + 1) * rows])
    return acc


def _colsum(x):
    return jnp.sum(_fold8(x, jnp.add), axis=0, keepdims=True)


def _colmax(x, rows=SUBLANES):
    return jnp.max(_fold8(x, jnp.maximum, rows), axis=0, keepdims=True)


def _sigmoid(x):
    return 1.0 / (1.0 + jnp.exp(-x))


def _absorb_kernel(wuk_ref, wuq_ref, o_ref):
    a = wuk_ref[0]
    b = wuq_ref[0]
    ah, al = _split2(a)
    bh, bl = _split2(b)
    acc = _dot_nt(ah, bh) + _dot_nt(ah, bl) + _dot_nt(al, bh)
    o_ref[0] = acc * np.float32(HEAD_DIM ** -0.5 * np.log2(np.e))


def _absorb(w_uk, w_uq_heads):
    return pl.pallas_call(
        _absorb_kernel,
        grid=(ATTN_HEADS,),
        in_specs=[pl.BlockSpec((1, KV_RANK, HEAD_DIM), lambda h: (h, 0, 0)),
                  pl.BlockSpec((1, Q_RANK, HEAD_DIM), lambda h: (h, 0, 0))],
        out_specs=pl.BlockSpec((1, KV_RANK, Q_RANK), lambda h: (h, 0, 0)),
        out_shape=jax.ShapeDtypeStruct((ATTN_HEADS, KV_RANK, Q_RANK), F32),
        compiler_params=_params(("arbitrary",)),
        name="absorb",
    )(w_uk, w_uq_heads)


def _inproj_kernel(x_ref, g_ref, w_ref, o_ref):
    x = x_ref[...]
    ms = jnp.mean(x * x, axis=-1, keepdims=True)
    xn = x * lax.rsqrt(ms + NORM_EPS) * g_ref[...]
    o_ref[...] = _dot(xn.astype(BF16), w_ref[...])


def _inproj(x2, norm_g, w_all, tm):
    n, d = x2.shape
    return pl.pallas_call(
        _inproj_kernel,
        grid=(n // tm,),
        in_specs=[pl.BlockSpec((tm, d), lambda i: (i, 0)),
                  pl.BlockSpec((1, d), lambda i: (0, 0)),
                  pl.BlockSpec((d, P_COLS), lambda i: (0, 0))],
        out_specs=pl.BlockSpec((tm, P_COLS), lambda i: (i, 0)),
        out_shape=jax.ShapeDtypeStruct((n, P_COLS), F32),
        compiler_params=_params(("parallel",)),
        name="inproj",
    )(x2, norm_g, w_all)


def _rwkv_kernel(rkv_ref, sm_ref, gate_ref, mu_rkv_ref, mu_sm_ref, w0_ref, wup_ref, a0_ref,
                 aup_ref, kkw_ref, ka_ref, rk_ref, gng_ref, gnb_ref, bd_ref, tri_ref,
                 o_ref,
                 carry_rkv, carry_sm, s_scr, n_scr, r_scr, u_scr, k_scr, v_scr, wc_scr, bv_scr,
                 t_scr, ankv_scr, ar_scr, *, tb):
    t = pl.program_id(1)
    W = RWKV_WIDTH

    @pl.when(t == 0)
    def _():
        carry_rkv[...] = jnp.zeros_like(carry_rkv)
        carry_sm[...] = jnp.zeros_like(carry_sm)
        s_scr[...] = jnp.zeros_like(s_scr)

    bd = bd_ref[...]
    half = W // 2

    def head_sum(x):
        st = jnp.concatenate([x[:, :half], x[:, half:]], axis=0)
        res = _dot(st.astype(BF16), bd)
        return jnp.concatenate([res[:CHUNK], res[CHUNK:]], axis=1)

    def shifted(ref, carry, c, rows):
        cur = ref[rows, :]
        first = carry[0:1, :] if c == 0 else ref[c * CHUNK - 1:c * CHUNK, :]
        row = lax.broadcasted_iota(I32, (CHUNK, 1), 0)
        return cur, jnp.where(row == 0, first, pltpu.roll(cur, 1, 0))

    def prep(c):
        rows = pl.ds(c * CHUNK, CHUNK)
        p, prev = shifted(rkv_ref, carry_rkv, c, rows)
        xs = p + (prev - p) * mu_rkv_ref[...]
        r = xs[:, 0:W]
        k = xs[:, W:2 * W]
        v = xs[:, 2 * W:3 * W]
        ps, prev_s = shifted(sm_ref, carry_sm, c, rows)
        xl = ps + (prev_s - ps) * mu_sm_ref[...]
        zw = w0_ref[...] + _dot(jnp.tanh(xl).astype(BF16), wup_ref[...])
        za = a0_ref[...] + _dot(xl.astype(BF16), aup_ref[...])
        yield
        logw = -np.float32(np.exp(-0.5)) * _sigmoid(zw)
        a = _sigmoid(za)
        kk = k * kkw_ref[...]
        ss = head_sum(kk * kk)
        hi, lo = _split2(logw)
        tri = tri_ref[...]
        cum = _dot(tri, hi) + _dot(tri, lo)
        yield
        kk = kk * (1.0 / jnp.maximum(jnp.sqrt(ss), 1e-12))
        k2 = k * (1.0 + (a - 1.0) * ka_ref[...])
        bonus = head_sum(r * k2 * rk_ref[...])
        wt = jnp.exp(cum)
        wi = jnp.exp(-cum)
        wx = jnp.exp(cum - logw)
        n_scr[rows, :] = (-kk * wx).astype(BF16)
        r_scr[rows, :] = (r * wt).astype(BF16)
        u_scr[rows, :] = (kk * a * wi).astype(BF16)
        k_scr[rows, :] = (k2 * wi).astype(BF16)
        v_scr[rows, :] = v.astype(BF16)
        wc_scr[c] = wt[CHUNK - SUBLANES:CHUNK, :]
        yield
        bv_scr[rows, :] = bonus * v

    rowp = lax.broadcasted_iota(I32, (PAIR, PAIR), 0)
    colp = lax.broadcasted_iota(I32, (PAIR, PAIR), 1)
    same_blk = (rowp >= HEAD_DIM) == (colp >= HEAD_DIM)
    strict_lo = (colp % HEAD_DIM) < (rowp % HEAD_DIM)
    m_abd = same_blk & strict_lo
    m_ank = jnp.logical_not(same_blk) & strict_lo
    rowc = lax.broadcasted_iota(I32, (CHUNK, PAIR), 0)
    colc = lax.broadcasted_iota(I32, (CHUNK, PAIR), 1)
    incl_lo2 = (lax.broadcasted_iota(I32, (CHUNK, 2 * PAIR), 1) % HEAD_DIM) <= \
        lax.broadcasted_iota(I32, (CHUNK, 2 * PAIR), 0)
    head0_c = colc < HEAD_DIM
    head0_p = colp < HEAD_DIM

    pairs = range(RWKV_HEADS // 2)
    lanes = [slice(g * PAIR, (g + 1) * PAIR) for g in pairs]
    eye = (rowp == colp).astype(F32)

    def phase_a(chunks):
        chains = [(c, g) for c in chunks for g in pairs]
        gtop = []
        for c, g in chains:
            rows = pl.ds(c * CHUNK, CHUNK)
            sl = lanes[g]
            nr = jnp.concatenate([n_scr[rows, sl], r_scr[rows, sl]], axis=0)
            uu = u_scr[rows, sl]
            kc = k_scr[rows, sl]
            zero = jnp.zeros_like(nr)
            rhs = jnp.concatenate(
                [jnp.where(head0_p, jnp.concatenate([uu, kc], axis=0), zero),
                 jnp.where(head0_p, zero, jnp.concatenate([kc, uu], axis=0))], axis=0)
            gg = _dot_nt(nr, rhs)
            ar_scr[c, g] = jnp.where(incl_lo2, gg[CHUNK:PAIR], 0.0).astype(BF16)
            gtop.append(jnp.concatenate([gg[0:CHUNK, 0:PAIR], gg[0:CHUNK, PAIR:]], axis=0))
        yield
        for i, (c, g) in enumerate(chains):
            vv = v_scr[pl.ds(c * CHUNK, CHUNK), lanes[g]]
            ank = jnp.where(m_ank, gtop[i], 0.0).astype(BF16)
            ankv_scr[c, g] = _dot(ank, jnp.concatenate([vv, vv], axis=0))
        n = range(len(chains))
        pw = [jnp.where(m_abd, gtop[i], 0.0) for i in n]
        tm = [eye + pw[i] for i in n]
        pwb = [x.astype(BF16) for x in pw]
        pw = [_dot(x, x) for x in pwb]
        yield
        for _ in range(4):
            pwb = [x.astype(BF16) for x in pw]
            both = [_dot(pwb[i], jnp.concatenate([pwb[i], tm[i].astype(BF16)], axis=1))
                    for i in n]
            pw = [x[:, 0:PAIR] for x in both]
            tm = [tm[i] + both[i][:, PAIR:] for i in n]
            yield
        tm = [tm[i] + _dot(pw[i].astype(BF16), tm[i].astype(BF16)) for i in n]
        for i, (c, g) in enumerate(chains):
            t_scr[c, g] = tm[i].astype(BF16)

    def phase_b(c):
        rows = pl.ds(c * CHUNK, CHUNK)
        wc_all = wc_scr[c][SUBLANES - 1:SUBLANES, :]
        s_old = [s_scr[g] for g in pairs]
        nrh = [_dot_nt(jnp.concatenate([n_scr[rows, lanes[g]], r_scr[rows, lanes[g]]], axis=0),
                       s_old[g].astype(BF16)) for g in pairs]
        yield
        zf = []
        for g in pairs:
            nh = nrh[g][0:CHUNK]
            bf = jnp.concatenate([nh, nh], axis=0) + ankv_scr[c, g]
            zf.append(_dot(t_scr[c, g], bf.astype(BF16)))
        yield
        ys = []
        for g in pairs:
            sl = lanes[g]
            vv = v_scr[rows, sl]
            zb = jnp.where(head0_c, zf[g][0:CHUNK], zf[g][CHUNK:PAIR]).astype(BF16)
            zv = jnp.concatenate([zb, vv], axis=0)
            vz = jnp.concatenate([vv, zb], axis=0)
            zero = jnp.zeros_like(zv)
            rhs = jnp.concatenate([jnp.where(head0_p, zv, zero), jnp.where(head0_p, zero, vz)],
                                  axis=0)
            ys.append(nrh[g][CHUNK:PAIR] + _dot(ar_scr[c, g], rhs))
            uk = jnp.concatenate([u_scr[rows, sl], k_scr[rows, sl]], axis=0)
            s_new = (s_old[g] + _dot_tn(zv, uk)) * wc_all[:, sl]
            s_scr[g] = jnp.where(same_blk, s_new, 0.0)
        yield
        y = jnp.concatenate(ys, axis=1)
        inv_d = np.float32(1.0 / HEAD_DIM)
        yc = y - head_sum(y) * inv_d
        yield
        var = head_sum(yc * yc) * inv_d
        y = yc * lax.rsqrt(var + GN_EPS) * gng_ref[...] + gnb_ref[...] + bv_scr[rows, :]
        gt = gate_ref[rows, :]
        o_ref[rows, :] = (y * (gt * _sigmoid(gt))).astype(o_ref.dtype)

    def in_step(gens):
        gens = list(gens)
        while gens:
            gens = [g for g in gens if next(g, True) is None]
            if gens:
                yield

    def in_turn(gens):
        for g in gens:
            yield from g

    nchunks = tb // CHUNK
    groups = [list(range(c0, c0 + A_GROUP)) for c0 in range(0, nchunks, A_GROUP)]
    for s in range(len(groups) + 2):
        streams = []
        if s - 2 >= 0:
            streams.append(in_turn(phase_b(c) for c in groups[s - 2]))
        if 0 <= s - 1 < len(groups):
            streams.append(phase_a(groups[s - 1]))
        if s < len(groups):
            streams.append(in_step(prep(c) for c in groups[s]))
        while streams:
            streams = [g for g in streams if next(g, True) is None]
    carry_rkv[0:1, :] = rkv_ref[tb - 1:tb, :]
    carry_sm[0:1, :] = sm_ref[tb - 1:tb, :]


def _rwkv(p, b, t, tb, consts):
    nt = t // tb
    W = RWKV_WIDTH
    row_spec = lambda width, cb: pl.BlockSpec((tb, width), lambda bi, ti: (bi * nt + ti, cb))
    full = lambda arr: pl.BlockSpec(arr.shape, lambda bi, ti: (0,) * arr.ndim)
    in_specs = [row_spec(3 * W, C_RKV // (3 * W)), row_spec(LANES, C_SM // LANES),
                row_spec(W, C_GR // W)] + [full(c) for c in consts]
    return pl.pallas_call(
        functools.partial(_rwkv_kernel, tb=tb),
        grid=(b, nt),
        in_specs=in_specs,
        out_specs=pl.BlockSpec((tb, W), lambda bi, ti: (bi * nt + ti, 0)),
        out_shape=jax.ShapeDtypeStruct((b * t, W), BF16),
        scratch_shapes=[pltpu.VMEM((SUBLANES, 3 * W), F32), pltpu.VMEM((SUBLANES, LANES), F32),
                        pltpu.VMEM((RWKV_HEADS // 2, PAIR, PAIR), F32),
                        pltpu.VMEM((tb, W), BF16), pltpu.VMEM((tb, W), BF16),
                        pltpu.VMEM((tb, W), BF16), pltpu.VMEM((tb, W), BF16),
                        pltpu.VMEM((tb, W), BF16),
                        pltpu.VMEM((tb // CHUNK, SUBLANES, W), F32),
                        pltpu.VMEM((tb, W), F32),
                        pltpu.VMEM((tb // CHUNK, RWKV_HEADS // 2, PAIR, PAIR), BF16),
                        pltpu.VMEM((tb // CHUNK, RWKV_HEADS // 2, PAIR, PAIR), F32),
                        pltpu.VMEM((tb // CHUNK, RWKV_HEADS // 2, CHUNK, 2 * PAIR), BF16)],
        compiler_params=_params(("parallel", "arbitrary")),
        name="rwkv",
    )(p, p, p, *consts)


def _dsaprep_kernel(qd_ref, kv_ref, sm_ref, wi_ref, qg_ref, kvg_ref, lng_ref, lnb_ref,
                    wqidx_ref, wabs_ref, wuv_ref,
                    ckv_ref, kidx_ref, vt_ref, qidx_ref, qabs_ref, wit_ref, *, tb):
    nq = tb // QB
    qd = qd_ref[...]
    cq = qd * lax.rsqrt(jnp.mean(qd * qd, axis=-1, keepdims=True) + NORM_EPS) * qg_ref[...]
    cqb = cq.astype(BF16)
    kv = kv_ref[...]
    ckv = kv * lax.rsqrt(jnp.mean(kv * kv, axis=-1, keepdims=True) + NORM_EPS) * kvg_ref[...]
    ckvb = ckv.astype(BF16)

    sm = sm_ref[...]
    lane = lax.broadcasted_iota(I32, sm.shape, 1)
    hi_half = lane >= IDX_DIM
    inv_d = np.float32(1.0 / IDX_DIM)
    mu = jnp.sum(jnp.where(hi_half, sm, 0.0), axis=-1, keepdims=True) * inv_d
    xc = jnp.where(hi_half, sm - mu, 0.0)
    var = jnp.sum(xc * xc, axis=-1, keepdims=True) * inv_d
    kidx = (xc * lax.rsqrt(var + LN_EPS) * lng_ref[...] + lnb_ref[...]).astype(BF16)

    vt = _dot_nt(wuv_ref[...], ckvb).astype(BF16)
    ones = jnp.ones((VROWS - HEAD_DIM, tb), BF16)
    vt = jnp.concatenate(
        [x for h in range(ATTN_HEADS) for x in (vt[h * HEAD_DIM:(h + 1) * HEAD_DIM], ones)],
        axis=0)
    wit = jnp.transpose(wi_ref[...])[0:SUBLANES, :] * np.float32(IDX_HEADS ** -0.5)
    for j in range(tb // KB):
        kidx_ref[0, j] = kidx[j * KB:(j + 1) * KB, :]
    for j in range(nq):
        ckv_ref[0, j] = ckvb[j * QB:(j + 1) * QB, :]
        vt_ref[0, j] = vt[:, j * QB:(j + 1) * QB]
    wit_ref[0] = wit
    for h in range(IDX_HEADS):
        res = _dot_nt(wqidx_ref[h], cqb).astype(BF16)
        for j in range(nq):
            qidx_ref[0, :, (j * IDX_HEADS + h) * QB:(j * IDX_HEADS + h + 1) * QB] = \
                res[:, j * QB:(j + 1) * QB]
    for h in range(ATTN_HEADS):
        res = _dot_nt(wabs_ref[h].astype(BF16), cqb).astype(BF16)
        for j in range(nq):
            qabs_ref[0, :, (j * ATTN_HEADS + h) * QB:(j * ATTN_HEADS + h + 1) * QB] = \
                res[:, j * QB:(j + 1) * QB]


def _dsaprep(p, b, t, tb, consts):
    nt = t // tb
    nb = t // QB
    nq = tb // QB
    row_spec = lambda width, cb: pl.BlockSpec((tb, width), lambda bi, ti: (bi * nt + ti, cb))
    full = lambda arr: pl.BlockSpec(arr.shape, lambda bi, ti: (0,) * arr.ndim)
    in_specs = [row_spec(Q_RANK, C_QD // Q_RANK), row_spec(KV_RANK, C_KV // KV_RANK),
                row_spec(LANES, C_SM // LANES), row_spec(LANES, C_WI // LANES)] + \
               [full(c) for c in consts]
    nkb = t // KB
    nk = tb // KB
    out_shape = [jax.ShapeDtypeStruct((b, nb, QB, KV_RANK), BF16),
                 jax.ShapeDtypeStruct((b, nkb, KB, LANES), BF16),
                 jax.ShapeDtypeStruct((b, nb, ATTN_HEADS * VROWS, QB), BF16),
                 jax.ShapeDtypeStruct((b, LANES, nb * IDX_HEADS * QB), BF16),
                 jax.ShapeDtypeStruct((b, KV_RANK, nb * ATTN_HEADS * QB), BF16),
                 jax.ShapeDtypeStruct((b, SUBLANES, t), F32)]
    out_specs = [pl.BlockSpec((1, nq, QB, KV_RANK), lambda bi, ti: (bi, ti, 0, 0)),
                 pl.BlockSpec((1, nk, KB, LANES), lambda bi, ti: (bi, ti, 0, 0)),
                 pl.BlockSpec((1, nq, ATTN_HEADS * VROWS, QB), lambda bi, ti: (bi, ti, 0, 0)),
                 pl.BlockSpec((1, LANES, nq * IDX_HEADS * QB), lambda bi, ti: (bi, 0, ti)),
                 pl.BlockSpec((1, KV_RANK, nq * ATTN_HEADS * QB), lambda bi, ti: (bi, 0, ti)),
                 pl.BlockSpec((1, SUBLANES, tb), lambda bi, ti: (bi, 0, ti))]
    return pl.pallas_call(
        functools.partial(_dsaprep_kernel, tb=tb),
        grid=(b, nt),
        in_specs=in_specs,
        out_specs=out_specs,
        out_shape=out_shape,
        compiler_params=_params(("parallel", "parallel")),
        name="dsaprep",
    )(p, p, p, p, *consts)


INT_MIN = np.int32(-2 ** 31)
KEY_NEG_INF = np.int32(np.array(0xFF800000, np.uint32).view(np.int32) ^ np.int32(0x7FFFFFFF))


def _dsa_kernel(kidx_ref, ckv_ref, vt_ref, qidx_ref, qabs_ref, wi_ref, gate_ref, tri_ref,
                o_ref, sc_scr, scb_scr, acc_scr, *, topk):
    i = pl.program_id(1)
    nkb = i // (KB // QB) + 1
    row = lax.broadcasted_iota(I32, (KB, QB), 0)
    col = lax.broadcasted_iota(I32, (KB, QB), 1)
    qpos = i * QB + col
    wi = wi_ref[0]

    def score_body(j, carry):
        lg = _dot(kidx_ref[0, j], qidx_ref[0])
        sc = jnp.zeros((KB, QB), F32)
        for h in range(IDX_HEADS):
            sc = sc + wi[h:h + 1, :] * jnp.maximum(lg[:, h * QB:(h + 1) * QB], 0.0)
        sc = jnp.where(j * KB + row <= qpos, sc, -jnp.inf)
        sc_scr[2 * j] = sc[0:QB]
        sc_scr[2 * j + 1] = sc[QB:KB]
        scb = sc.astype(BF16)
        scb_scr[2 * j] = scb[0:QB]
        scb_scr[2 * j + 1] = scb[QB:KB]
        return carry

    lax.fori_loop(0, nkb, score_body, 0)

    def count(pred):
        def body(j, accs):
            a0, a1 = accs
            a0 = a0 + _fold8(jnp.where(pred(sc_scr[2 * j]), 1, 0), jnp.add)
            a1 = a1 + _fold8(jnp.where(pred(sc_scr[2 * j + 1]), 1, 0), jnp.add)
            return a0, a1
        zero = jnp.zeros((SUBLANES, QB), I32)
        a0, a1 = lax.fori_loop(0, nkb, body, (zero, zero))
        return jnp.sum(a0 + a1, axis=0, keepdims=True)

    def key_value(key):
        return lax.bitcast_convert_type(jnp.where(key < 0, key ^ np.int32(0x7FFFFFFF), key), F32)

    def count_bf16(cand):
        one = jnp.ones((QB, QB), BF16)
        zero = jnp.zeros((QB, QB), BF16)

        def body(j, acc):
            for c in (2 * j, 2 * j + 1):
                hits = jnp.where(scb_scr[c] >= cand, one, zero)
                acc = acc + _fold8(hits, jnp.add, rows=BF16_ROWS).astype(F32)
            return acc
        acc = lax.fori_loop(0, nkb, body, jnp.zeros((BF16_ROWS, QB), F32))
        return jnp.sum(acc, axis=0, keepdims=True).astype(I32)

    top_key = jnp.full((1, QB), INT_MIN, I32)
    for bit in range(31, 15, -1):
        cand = jnp.zeros((1, QB), I32) if bit == 31 else top_key | np.int32(1 << bit)
        cnt = count_bf16(key_value(cand).astype(BF16))
        top_key = jnp.where(cnt >= topk, cand, top_key)
    bf16_step = np.int32(1 << 16)
    thr_key = jnp.maximum(top_key, INT_MIN + bf16_step) - bf16_step
    hi_key = top_key + (bf16_step - 1)
    for _ in range(17):
        mid = thr_key + ((hi_key - thr_key + 1) >> 1)
        ok = count(lambda s, v=key_value(mid): s >= v) >= topk
        thr_key = jnp.where(ok, mid, thr_key)
        hi_key = jnp.where(ok, hi_key, mid - 1)
    thr = jnp.where(thr_key <= KEY_NEG_INF, -jnp.inf, key_value(thr_key))
    need = (topk - count(lambda s: s > thr)).astype(F32)

    tri = tri_ref[...]

    acc_scr[...] = jnp.zeros_like(acc_scr)

    def attend(c, carry):
        run, ms = carry
        key = sc_scr[c]
        eq = key == thr
        eqf = jnp.where(eq, 1.0, 0.0)
        before = _dot(tri, eqf.astype(BF16)) + run
        sel = (key > thr) | (eq & (before < need))
        sel = sel & (c * QB + lax.broadcasted_iota(I32, (QB, QB), 0) <=
                     i * QB + lax.broadcasted_iota(I32, (QB, QB), 1))
        bias = jnp.where(sel, 0.0, -jnp.inf)
        run = run + _colsum(eqf)
        heads = range(ATTN_HEADS)
        s = _dot(ckv_ref[0, c], qabs_ref[0])
        bias_b = bias.astype(BF16)
        sh = [s[:, h * QB:(h + 1) * QB].astype(BF16) + bias_b for h in heads]
        new_ms = [jnp.maximum(ms[h], _colmax(sh[h], rows=BF16_ROWS).astype(F32)) for h in heads]
        shift = [jnp.where(m == -jnp.inf, 0.0, m) for m in new_ms]
        ps = [jnp.exp2(sh[h] - shift[h].astype(BF16)) for h in heads]
        outs = [_dot(vt_ref[0, c, h * VROWS:(h + 1) * VROWS, :], ps[h]) for h in heads]
        scales = [jnp.broadcast_to(jnp.exp2(ms[h] - shift[h]), (VROWS, QB)) for h in heads]
        acc_scr[...] = acc_scr[...] * jnp.concatenate(scales, axis=0) + \
            jnp.concatenate(outs, axis=0)
        return run, tuple(new_ms)

    ninf = jnp.full((1, QB), -jnp.inf, F32)
    lax.fori_loop(0, i + 1, attend,
                  (jnp.zeros((1, QB), F32), tuple(ninf for _ in range(ATTN_HEADS))))

    o = jnp.concatenate(
        [acc_scr[h * VROWS:h * VROWS + HEAD_DIM, :] *
         (1.0 / acc_scr[h * VROWS + HEAD_DIM:h * VROWS + HEAD_DIM + 1, :])
         for h in range(ATTN_HEADS)], axis=0)
    o = jnp.transpose(o)
    gt = gate_ref[...]
    o_ref[...] = (o * (gt * _sigmoid(gt))).astype(o_ref.dtype)


def _dsa(p, prep, tri, b, t, topk):
    nb = t // QB
    ckv, kidx, vt, qidx, qabs, wit = prep
    nkb = t // KB
    in_specs = [pl.BlockSpec((1, nkb, KB, LANES), lambda bi, qi: (bi, 0, 0, 0)),
                pl.BlockSpec((1, nb, QB, KV_RANK), lambda bi, qi: (bi, 0, 0, 0)),
                pl.BlockSpec((1, nb, ATTN_HEADS * VROWS, QB), lambda bi, qi: (bi, 0, 0, 0)),
                pl.BlockSpec((1, LANES, IDX_HEADS * QB), lambda bi, qi: (bi, 0, qi)),
                pl.BlockSpec((1, KV_RANK, ATTN_HEADS * QB), lambda bi, qi: (bi, 0, qi)),
                pl.BlockSpec((1, SUBLANES, QB), lambda bi, qi: (bi, 0, qi)),
                pl.BlockSpec((QB, ATTN_WIDTH), lambda bi, qi: (bi * nb + qi, C_GA // ATTN_WIDTH)),
                pl.BlockSpec((QB, QB), lambda bi, qi: (0, 0))]
    return pl.pallas_call(
        functools.partial(_dsa_kernel, topk=topk),
        grid=(b, nb),
        in_specs=in_specs,
        out_specs=pl.BlockSpec((QB, ATTN_WIDTH), lambda bi, qi: (bi * nb + qi, 0)),
        out_shape=jax.ShapeDtypeStruct((b * t, ATTN_WIDTH), BF16),
        scratch_shapes=[pltpu.VMEM((nb, QB, QB), F32), pltpu.VMEM((nb, QB, QB), BF16),
                        pltpu.VMEM((ATTN_HEADS * VROWS, QB), F32)],
        compiler_params=_params(("parallel", "arbitrary")),
        name="dsa",
    )(kidx, ckv, vt, qidx, qabs, wit, p, tri)


def _outproj_kernel(yr_ref, ya_ref, x_ref, w_ref, g_ref, o_ref):
    w = w_ref[...]
    y = _dot(yr_ref[...], w[0:RWKV_WIDTH]) + _dot(ya_ref[...], w[RWKV_WIDTH:])
    z = x_ref[...] + y
    ms = jnp.mean(z * z, axis=-1, keepdims=True)
    o_ref[...] = z * lax.rsqrt(ms + NORM_EPS) * g_ref[...]


def _outproj(yr, ya, x2, w_out, final_g, tm):
    n, d = x2.shape
    return pl.pallas_call(
        _outproj_kernel,
        grid=(n // tm,),
        in_specs=[pl.BlockSpec((tm, RWKV_WIDTH), lambda i: (i, 0)),
                  pl.BlockSpec((tm, ATTN_WIDTH), lambda i: (i, 0)),
                  pl.BlockSpec((tm, d), lambda i: (i, 0)),
                  pl.BlockSpec(w_out.shape, lambda i: (0, 0)),
                  pl.BlockSpec((1, d), lambda i: (0, 0))],
        out_specs=pl.BlockSpec((tm, d), lambda i: (i, 0)),
        out_shape=jax.ShapeDtypeStruct((n, d), F32),
        compiler_params=_params(("parallel",)),
        name="outproj",
    )(yr, ya, x2, w_out, final_g)


class _BlockRows(NamedTuple):
    inproj_rows: int
    rwkv_rows: int
    outproj_rows: int
    dsaprep_rows: int


def _block_rows(b, t):
    n = b * t
    assert t % KB == 0 and t % (A_GROUP * CHUNK) == 0
    pick = lambda want, total: want if total % want == 0 else KB
    return _BlockRows(inproj_rows=pick(512, n), rwkv_rows=pick(1024, t),
                      outproj_rows=pick(1024, n), dsaprep_rows=pick(1024, t))


def _pad_rows(w, start, total):
    return jnp.zeros((total, w.shape[1]), w.dtype).at[start:start + w.shape[0]].set(w)


def _layer(x2, b, t, norm_g, w_in, mu_shift, w0, w_up, a0, a_up, k_k, k_a, r_k, gn_g, gn_b,
           q_norm_g, kv_norm_g, w_uq, w_uk, w_uv, w_qidx, kidx_g, kidx_b, w_out, final_g):
    W = RWKV_WIDTH
    d = x2.shape[1]
    o_r, o_k, o_v = 0, W, 2 * W
    o_wd = 3 * W
    o_ad = o_wd + DECAY_LORA
    o_gr = o_ad + AAA_LORA
    o_qd = o_gr + W
    o_kv = o_qd + Q_RANK
    o_ki = o_kv + KV_RANK
    o_wi = o_ki + IDX_DIM
    o_ga = o_wi + IDX_HEADS
    cols = lambda s, n: w_in[:, s:s + n]
    w_all = jnp.concatenate(
        [cols(o_r, 3 * W), cols(o_gr, W), cols(o_ga, ATTN_WIDTH), cols(o_qd, Q_RANK),
         cols(o_kv, KV_RANK), cols(o_wd, DECAY_LORA), cols(o_ad, AAA_LORA), cols(o_ki, IDX_DIM),
         cols(o_wi, IDX_HEADS), jnp.zeros((d, LANES - IDX_HEADS), w_in.dtype)],
        axis=1).astype(BF16)

    row2 = lambda v: v.reshape(1, -1).astype(F32)
    mu_rkv = row2(mu_shift[0:3 * W])
    mu_sm = row2(jnp.concatenate([mu_shift[3 * W:], jnp.zeros((IDX_DIM,), F32)]))
    wup_pad = _pad_rows(w_up, 0, LANES).astype(BF16)
    aup_pad = _pad_rows(a_up, DECAY_LORA, LANES).astype(BF16)
    hid = np.arange(W // 2) // HEAD_DIM
    bd = jnp.asarray((hid[:, None] == hid[None, :]).astype(np.float32), dtype=BF16)

    blk = _block_rows(b, t)
    ti = np.arange(CHUNK)
    tri_r = jnp.asarray((ti[None, :] <= ti[:, None]).astype(np.float32), dtype=BF16)
    rwkv_consts = [mu_rkv, mu_sm, row2(w0), wup_pad, row2(a0), aup_pad, row2(k_k), row2(k_a),
                   row2(r_k), row2(gn_g), row2(gn_b), bd, tri_r]

    wq_t = jnp.transpose(w_qidx.reshape(Q_RANK, IDX_HEADS, IDX_DIM), (1, 2, 0))
    wq_t = jnp.concatenate([jnp.zeros_like(wq_t), wq_t], axis=1).astype(BF16)
    w_uq_heads = jnp.transpose(w_uq.reshape(Q_RANK, ATTN_HEADS, HEAD_DIM), (1, 0, 2))
    wabs_t = _absorb(w_uk, w_uq_heads)
    wuv_t = jnp.transpose(w_uv, (0, 2, 1)).reshape(ATTN_WIDTH, KV_RANK).astype(BF16)
    lng = row2(jnp.concatenate([jnp.zeros((IDX_DIM,), F32), kidx_g]))
    lnb = row2(jnp.concatenate([jnp.zeros((IDX_DIM,), F32), kidx_b]))
    prep_consts = [row2(q_norm_g), row2(kv_norm_g), lng, lnb, wq_t, wabs_t, wuv_t]
    ki = np.arange(QB)
    tri_q = jnp.asarray((ki[None, :] < ki[:, None]).astype(np.float32), dtype=BF16)

    p = _inproj(x2, row2(norm_g), w_all, blk.inproj_rows)
    y_r = _rwkv(p, b, t, blk.rwkv_rows, rwkv_consts)
    prep = _dsaprep(p, b, t, blk.dsaprep_rows, prep_consts)
    y_a = _dsa(p, prep, tri_q, b, t, min(TOPK_MAX, t // 4))
    return _outproj(y_r, y_a, x2, w_out.astype(BF16), row2(final_g), blk.outproj_rows)


def kernel(x, norm_g, w_in, mu_shift, w0, w_up, a0, a_up, k_k, k_a, r_k, gn_g, gn_b,
           q_norm_g, kv_norm_g, w_uq, w_uk, w_uv, w_qidx, kidx_g, kidx_b, w_out, final_g):
    b, t, d = x.shape
    assert norm_g.shape[0] == 1, "single-layer problem"
    out = _layer(x.reshape(b * t, d), b, t, norm_g[0], w_in[0], mu_shift[0], w0[0], w_up[0],
                 a0[0], a_up[0], k_k[0], k_a[0], r_k[0], gn_g[0], gn_b[0], q_norm_g[0],
                 kv_norm_g[0], w_uq[0], w_uk[0], w_uv[0], w_qidx[0], kidx_g[0], kidx_b[0],
                 w_out[0], final_g)
    return out.reshape(b, t, d)
```

```python
import functools
from typing import NamedTuple

import numpy as np
import jax
import jax.numpy as jnp
from jax import lax
from jax.experimental import pallas as pl
from jax.experimental.pallas import tpu as pltpu

F32 = jnp.float32
BF16 = jnp.bfloat16
I32 = jnp.int32

HEAD_DIM = 64
RWKV_HEADS = 8
RWKV_WIDTH = RWKV_HEADS * HEAD_DIM
DECAY_LORA = 32
AAA_LORA = 32
ATTN_HEADS = 8
ATTN_WIDTH = ATTN_HEADS * HEAD_DIM
Q_RANK = 256
KV_RANK = 128
IDX_HEADS = 4
IDX_DIM = 64
TOPK_MAX = 256
NORM_EPS = 1e-6
LN_EPS = 1e-5
GN_EPS = 64e-5

LANES = 128
SUBLANES = 8
BF16_ROWS = 16
QB = 256
KB = 2 * QB
CHUNK = 64
PAIR = 2 * HEAD_DIM
A_GROUP = 2
VROWS = HEAD_DIM + BF16_ROWS

C_RKV = 0
C_GR = 3 * RWKV_WIDTH
C_GA = C_GR + RWKV_WIDTH
C_QD = C_GA + ATTN_WIDTH
C_KV = C_QD + Q_RANK
C_SM = C_KV + KV_RANK
C_WI = C_SM + LANES
P_COLS = C_WI + LANES

VMEM_LIMIT = 56 * 1024 * 1024


def _params(sem):
    return pltpu.CompilerParams(dimension_semantics=sem, vmem_limit_bytes=VMEM_LIMIT)


def _dot(a, b):
    return jnp.dot(a, b, preferred_element_type=F32)


def _dot_nt(a, b):
    return lax.dot_general(a, b, (((1,), (1,)), ((), ())), preferred_element_type=F32)


def _dot_tn(a, b):
    return lax.dot_general(a, b, (((0,), (0,)), ((), ())), preferred_element_type=F32)


def _split2(x):
    hi = x.astype(BF16)
    lo = (x - hi.astype(F32)).astype(BF16)
    return hi, lo


def _fold8(x, op, rows=SUBLANES):
    acc = x[0:rows]
    for j in range(1, x.shape[0] // rows):
        acc = op(acc, x[j * rows:(j + 1) * rows])
    return acc


def _colsum(x):
    return jnp.sum(_fold8(x, jnp.add), axis=0, keepdims=True)


def _colmax(x, rows=SUBLANES):
    return jnp.max(_fold8(x, jnp.maximum, rows), axis=0, keepdims=True)


def _sigmoid(x):
    return 1.0 / (1.0 + jnp.exp(-x))


def _absorb_kernel(wuk_ref, wuq_ref, o_ref):
    a = wuk_ref[0]
    b = wuq_ref[0]
    ah, al = _split2(a)
    bh, bl = _split2(b)
    acc = _dot_nt(ah, bh) + _dot_nt(ah, bl) + _dot_nt(al, bh)
    o_ref[0] = acc * np.float32(HEAD_DIM ** -0.5 * np.log2(np.e))


def _absorb(w_uk, w_uq_heads):
    return pl.pallas_call(
        _absorb_kernel,
        grid=(ATTN_HEADS,),
        in_specs=[pl.BlockSpec((1, KV_RANK, HEAD_DIM), lambda h: (h, 0, 0)),
                  pl.BlockSpec((1, Q_RANK, HEAD_DIM), lambda h: (h, 0, 0))],
        out_specs=pl.BlockSpec((1, KV_RANK, Q_RANK), lambda h: (h, 0, 0)),
        out_shape=jax.ShapeDtypeStruct((ATTN_HEADS, KV_RANK, Q_RANK), F32),
        compiler_params=_params(("arbitrary",)),
        name="absorb",
    )(w_uk, w_uq_heads)


def _inproj_kernel(x_ref, g_ref, w_ref, o_ref):
    x = x_ref[...]
    ms = jnp.mean(x * x, axis=-1, keepdims=True)
    xn = x * lax.rsqrt(ms + NORM_EPS) * g_ref[...]
    o_ref[...] = _dot(xn.astype(BF16), w_ref[...])


def _inproj(x2, norm_g, w_all, tm):
    n, d = x2.shape
    return pl.pallas_call(
        _inproj_kernel,
        grid=(n // tm,),
        in_specs=[pl.BlockSpec((tm, d), lambda i: (i, 0)),
                  pl.BlockSpec((1, d), lambda i: (0, 0)),
                  pl.BlockSpec((d, P_COLS), lambda i: (0, 0))],
        out_specs=pl.BlockSpec((tm, P_COLS), lambda i: (i, 0)),
        out_shape=jax.ShapeDtypeStruct((n, P_COLS), F32),
        compiler_params=_params(("parallel",)),
        name="inproj",
    )(x2, norm_g, w_all)


def _rwkv_kernel(rkv_ref, sm_ref, gate_ref, mu_rkv_ref, mu_sm_ref, w0_ref, wup_ref, a0_ref,
                 aup_ref, kkw_ref, ka_ref, rk_ref, gng_ref, gnb_ref, bd_ref, tri_ref,
                 o_ref,
                 carry_rkv, carry_sm, s_scr, n_scr, r_scr, u_scr, k_scr, v_scr, wc_scr, bv_scr,
                 t_scr, ankv_scr, ar_scr, *, tb):
    t = pl.program_id(1)
    W = RWKV_WIDTH

    @pl.when(t == 0)
    def _():
        carry_rkv[...] = jnp.zeros_like(carry_rkv)
        carry_sm[...] = jnp.zeros_like(carry_sm)
        s_scr[...] = jnp.zeros_like(s_scr)

    bd = bd_ref[...]
    half = W // 2

    def head_sum(x):
        st = jnp.concatenate([x[:, :half], x[:, half:]], axis=0)
        res = _dot(st.astype(BF16), bd)
        return jnp.concatenate([res[:CHUNK], res[CHUNK:]], axis=1)

    def shifted(ref, carry, c, rows):
        cur = ref[rows, :]
        first = carry[0:1, :] if c == 0 else ref[c * CHUNK - 1:c * CHUNK, :]
        row = lax.broadcasted_iota(I32, (CHUNK, 1), 0)
        return cur, jnp.where(row == 0, first, pltpu.roll(cur, 1, 0))

    def prep(c):
        rows = pl.ds(c * CHUNK, CHUNK)
        p, prev = shifted(rkv_ref, carry_rkv, c, rows)
        xs = p + (prev - p) * mu_rkv_ref[...]
        r = xs[:, 0:W]
        k = xs[:, W:2 * W]
        v = xs[:, 2 * W:3 * W]
        ps, prev_s = shifted(sm_ref, carry_sm, c, rows)
        xl = ps + (prev_s - ps) * mu_sm_ref[...]
        zw = w0_ref[...] + _dot(jnp.tanh(xl).astype(BF16), wup_ref[...])
        za = a0_ref[...] + _dot(xl.astype(BF16), aup_ref[...])
        yield
        logw = -np.float32(np.exp(-0.5)) * _sigmoid(zw)
        a = _sigmoid(za)
        kk = k * kkw_ref[...]
        ss = head_sum(kk * kk)
        hi, lo = _split2(logw)
        tri = tri_ref[...]
        cum = _dot(tri, hi) + _dot(tri, lo)
        yield
        kk = kk * (1.0 / jnp.maximum(jnp.sqrt(ss), 1e-12))
        k2 = k * (1.0 + (a - 1.0) * ka_ref[...])
        bonus = head_sum(r * k2 * rk_ref[...])
        wt = jnp.exp(cum)
        wi = jnp.exp(-cum)
        wx = jnp.exp(cum - logw)
        n_scr[rows, :] = (-kk * wx).astype(BF16)
        r_scr[rows, :] = (r * wt).astype(BF16)
        u_scr[rows, :] = (kk * a * wi).astype(BF16)
        k_scr[rows, :] = (k2 * wi).astype(BF16)
        v_scr[rows, :] = v.astype(BF16)
        wc_scr[c] = wt[CHUNK - SUBLANES:CHUNK, :]
        yield
        bv_scr[rows, :] = bonus * v

    rowp = lax.broadcasted_iota(I32, (PAIR, PAIR), 0)
    colp = lax.broadcasted_iota(I32, (PAIR, PAIR), 1)
    same_blk = (rowp >= HEAD_DIM) == (colp >= HEAD_DIM)
    strict_lo = (colp % HEAD_DIM) < (rowp % HEAD_DIM)
    m_abd = same_blk & strict_lo
    m_ank = jnp.logical_not(same_blk) & strict_lo
    rowc = lax.broadcasted_iota(I32, (CHUNK, PAIR), 0)
    colc = lax.broadcasted_iota(I32, (CHUNK, PAIR), 1)
    incl_lo2 = (lax.broadcasted_iota(I32, (CHUNK, 2 * PAIR), 1) % HEAD_DIM) <= \
        lax.broadcasted_iota(I32, (CHUNK, 2 * PAIR), 0)
    head0_c = colc < HEAD_DIM
    head0_p = colp < HEAD_DIM

    pairs = range(RWKV_HEADS // 2)
    lanes = [slice(g * PAIR, (g + 1) * PAIR) for g in pairs]
    eye = (rowp == colp).astype(F32)

    def phase_a(chunks):
        chains = [(c, g) for c in chunks for g in pairs]
        gtop = []
        for c, g in chains:
            rows = pl.ds(c * CHUNK, CHUNK)
            sl = lanes[g]
            nr = jnp.concatenate([n_scr[rows, sl], r_scr[rows, sl]], axis=0)
            uu = u_scr[rows, sl]
            kc = k_scr[rows, sl]
            zero = jnp.zeros_like(nr)
            rhs = jnp.concatenate(
                [jnp.where(head0_p, jnp.concatenate([uu, kc], axis=0), zero),
                 jnp.where(head0_p, zero, jnp.concatenate([kc, uu], axis=0))], axis=0)
            gg = _dot_nt(nr, rhs)
            ar_scr[c, g] = jnp.where(incl_lo2, gg[CHUNK:PAIR], 0.0).astype(BF16)
            gtop.append(jnp.concatenate([gg[0:CHUNK, 0:PAIR], gg[0:CHUNK, PAIR:]], axis=0))
        yield
        for i, (c, g) in enumerate(chains):
            vv = v_scr[pl.ds(c * CHUNK, CHUNK), lanes[g]]
            ank = jnp.where(m_ank, gtop[i], 0.0).astype(BF16)
            ankv_scr[c, g] = _dot(ank, jnp.concatenate([vv, vv], axis=0))
        n = range(len(chains))
        pw = [jnp.where(m_abd, gtop[i], 0.0) for i in n]
        tm = [eye + pw[i] for i in n]
        pwb = [x.astype(BF16) for x in pw]
        pw = [_dot(x, x) for x in pwb]
        yield
        for _ in range(4):
            pwb = [x.astype(BF16) for x in pw]
            both = [_dot(pwb[i], jnp.concatenate([pwb[i], tm[i].astype(BF16)], axis=1))
                    for i in n]
            pw = [x[:, 0:PAIR] for x in both]
            tm = [tm[i] + both[i][:, PAIR:] for i in n]
            yield
        tm = [tm[i] + _dot(pw[i].astype(BF16), tm[i].astype(BF16)) for i in n]
        for i, (c, g) in enumerate(chains):
            t_scr[c, g] = tm[i].astype(BF16)

    def phase_b(c):
        rows = pl.ds(c * CHUNK, CHUNK)
        wc_all = wc_scr[c][SUBLANES - 1:SUBLANES, :]
        s_old = [s_scr[g] for g in pairs]
        nrh = [_dot_nt(jnp.concatenate([n_scr[rows, lanes[g]], r_scr[rows, lanes[g]]], axis=0),
                       s_old[g].astype(BF16)) for g in pairs]
        yield
        zf = []
        for g in pairs:
            nh = nrh[g][0:CHUNK]
            bf = jnp.concatenate([nh, nh], axis=0) + ankv_scr[c, g]
            zf.append(_dot(t_scr[c, g], bf.astype(BF16)))
        yield
        ys = []
        for g in pairs:
            sl = lanes[g]
            vv = v_scr[rows, sl]
            zb = jnp.where(head0_c, zf[g][0:CHUNK], zf[g][CHUNK:PAIR]).astype(BF16)
            zv = jnp.concatenate([zb, vv], axis=0)
            vz = jnp.concatenate([vv, zb], axis=0)
            zero = jnp.zeros_like(zv)
            rhs = jnp.concatenate([jnp.where(head0_p, zv, zero), jnp.where(head0_p, zero, vz)],
                                  axis=0)
            ys.append(nrh[g][CHUNK:PAIR] + _dot(ar_scr[c, g], rhs))
            uk = jnp.concatenate([u_scr[rows, sl], k_scr[rows, sl]], axis=0)
            s_new = (s_old[g] + _dot_tn(zv, uk)) * wc_all[:, sl]
            s_scr[g] = jnp.where(same_blk, s_new, 0.0)
        yield
        y = jnp.concatenate(ys, axis=1)
        inv_d = np.float32(1.0 / HEAD_DIM)
        yc = y - head_sum(y) * inv_d
        yield
        var = head_sum(yc * yc) * inv_d
        y = yc * lax.rsqrt(var + GN_EPS) * gng_ref[...] + gnb_ref[...] + bv_scr[rows, :]
        gt = gate_ref[rows, :]
        o_ref[rows, :] = (y * (gt * _sigmoid(gt))).astype(o_ref.dtype)

    def in_step(gens):
        gens = list(gens)
        while gens:
            gens = [g for g in gens if next(g, True) is None]
            if gens:
                yield

    def in_turn(gens):
        for g in gens:
            yield from g

    nchunks = tb // CHUNK
    groups = [list(range(c0, c0 + A_GROUP)) for c0 in range(0, nchunks, A_GROUP)]
    for s in range(len(groups) + 2):
        streams = []
        if s - 2 >= 0:
            streams.append(in_turn(phase_b(c) for c in groups[s - 2]))
        if 0 <= s - 1 < len(groups):
            streams.append(phase_a(groups[s - 1]))
        if s < len(groups):
            streams.append(in_step(prep(c) for c in groups[s]))
        while streams:
            streams = [g for g in streams if next(g, True) is None]
    carry_rkv[0:1, :] = rkv_ref[tb - 1:tb, :]
    carry_sm[0:1, :] = sm_ref[tb - 1:tb, :]


def _rwkv(p, b, t, tb, consts):
    nt = t // tb
    W = RWKV_WIDTH
    row_spec = lambda width, cb: pl.BlockSpec((tb, width), lambda bi, ti: (bi * nt + ti, cb))
    full = lambda arr: pl.BlockSpec(arr.shape, lambda bi, ti: (0,) * arr.ndim)
    in_specs = [row_spec(3 * W, C_RKV // (3 * W)), row_spec(LANES, C_SM // LANES),
                row_spec(W, C_GR // W)] + [full(c) for c in consts]
    return pl.pallas_call(
        functools.partial(_rwkv_kernel, tb=tb),
        grid=(b, nt),
        in_specs=in_specs,
        out_specs=pl.BlockSpec((tb, W), lambda bi, ti: (bi * nt + ti, 0)),
        out_shape=jax.ShapeDtypeStruct((b * t, W), BF16),
        scratch_shapes=[pltpu.VMEM((SUBLANES, 3 * W), F32), pltpu.VMEM((SUBLANES, LANES), F32),
                        pltpu.VMEM((RWKV_HEADS // 2, PAIR, PAIR), F32),
                        pltpu.VMEM((tb, W), BF16), pltpu.VMEM((tb, W), BF16),
                        pltpu.VMEM((tb, W), BF16), pltpu.VMEM((tb, W), BF16),
                        pltpu.VMEM((tb, W), BF16),
                        pltpu.VMEM((tb // CHUNK, SUBLANES, W), F32),
                        pltpu.VMEM((tb, W), F32),
                        pltpu.VMEM((tb // CHUNK, RWKV_HEADS // 2, PAIR, PAIR), BF16),
                        pltpu.VMEM((tb // CHUNK, RWKV_HEADS // 2, PAIR, PAIR), F32),
                        pltpu.VMEM((tb // CHUNK, RWKV_HEADS // 2, CHUNK, 2 * PAIR), BF16)],
        compiler_params=_params(("parallel", "arbitrary")),
        name="rwkv",
    )(p, p, p, *consts)


def _dsaprep_kernel(qd_ref, kv_ref, sm_ref, wi_ref, qg_ref, kvg_ref, lng_ref, lnb_ref,
                    wqidx_ref, wabs_ref, wuv_ref,
                    ckv_ref, kidx_ref, vt_ref, qidx_ref, qabs_ref, wit_ref, *, tb):
    nq = tb // QB
    qd = qd_ref[...]
    cq = qd * lax.rsqrt(jnp.mean(qd * qd, axis=-1, keepdims=True) + NORM_EPS) * qg_ref[...]
    cqb = cq.astype(BF16)
    kv = kv_ref[...]
    ckv = kv * lax.rsqrt(jnp.mean(kv * kv, axis=-1, keepdims=True) + NORM_EPS) * kvg_ref[...]
    ckvb = ckv.astype(BF16)

    sm = sm_ref[...]
    lane = lax.broadcasted_iota(I32, sm.shape, 1)
    hi_half = lane >= IDX_DIM
    inv_d = np.float32(1.0 / IDX_DIM)
    mu = jnp.sum(jnp.where(hi_half, sm, 0.0), axis=-1, keepdims=True) * inv_d
    xc = jnp.where(hi_half, sm - mu, 0.0)
    var = jnp.sum(xc * xc, axis=-1, keepdims=True) * inv_d
    kidx = (xc * lax.rsqrt(var + LN_EPS) * lng_ref[...] + lnb_ref[...]).astype(BF16)

    vt = _dot_nt(wuv_ref[...], ckvb).astype(BF16)
    ones = jnp.ones((VROWS - HEAD_DIM, tb), BF16)
    vt = jnp.concatenate(
        [x for h in range(ATTN_HEADS) for x in (vt[h * HEAD_DIM:(h + 1) * HEAD_DIM], ones)],
        axis=0)
    wit = jnp.transpose(wi_ref[...])[0:SUBLANES, :] * np.float32(IDX_HEADS ** -0.5)
    for j in range(tb // KB):
        kidx_ref[0, j] = kidx[j * KB:(j + 1) * KB, :]
    for j in range(nq):
        ckv_ref[0, j] = ckvb[j * QB:(j + 1) * QB, :]
        vt_ref[0, j] = vt[:, j * QB:(j + 1) * QB]
    wit_ref[0] = wit
    for h in range(IDX_HEADS):
        res = _dot_nt(wqidx_ref[h], cqb).astype(BF16)
        for j in range(nq):
            qidx_ref[0, :, (j * IDX_HEADS + h) * QB:(j * IDX_HEADS + h + 1) * QB] = \
                res[:, j * QB:(j + 1) * QB]
    for h in range(ATTN_HEADS):
        res = _dot_nt(wabs_ref[h].astype(BF16), cqb).astype(BF16)
        for j in range(nq):
            qabs_ref[0, :, (j * ATTN_HEADS + h) * QB:(j * ATTN_HEADS + h + 1) * QB] = \
                res[:, j * QB:(j + 1) * QB]


def _dsaprep(p, b, t, tb, consts):
    nt = t // tb
    nb = t // QB
    nq = tb // QB
    row_spec = lambda width, cb: pl.BlockSpec((tb, width), lambda bi, ti: (bi * nt + ti, cb))
    full = lambda arr: pl.BlockSpec(arr.shape, lambda bi, ti: (0,) * arr.ndim)
    in_specs = [row_spec(Q_RANK, C_QD // Q_RANK), row_spec(KV_RANK, C_KV // KV_RANK),
                row_spec(LANES, C_SM // LANES), row_spec(LANES, C_WI // LANES)] + \
               [full(c) for c in consts]
    nkb = t // KB
    nk = tb // KB
    out_shape = [jax.ShapeDtypeStruct((b, nb, QB, KV_RANK), BF16),
                 jax.ShapeDtypeStruct((b, nkb, KB, LANES), BF16),
                 jax.ShapeDtypeStruct((b, nb, ATTN_HEADS * VROWS, QB), BF16),
                 jax.ShapeDtypeStruct((b, LANES, nb * IDX_HEADS * QB), BF16),
                 jax.ShapeDtypeStruct((b, KV_RANK, nb * ATTN_HEADS * QB), BF16),
                 jax.ShapeDtypeStruct((b, SUBLANES, t), F32)]
    out_specs = [pl.BlockSpec((1, nq, QB, KV_RANK), lambda bi, ti: (bi, ti, 0, 0)),
                 pl.BlockSpec((1, nk, KB, LANES), lambda bi, ti: (bi, ti, 0, 0)),
                 pl.BlockSpec((1, nq, ATTN_HEADS * VROWS, QB), lambda bi, ti: (bi, ti, 0, 0)),
                 pl.BlockSpec((1, LANES, nq * IDX_HEADS * QB), lambda bi, ti: (bi, 0, ti)),
                 pl.BlockSpec((1, KV_RANK, nq * ATTN_HEADS * QB), lambda bi, ti: (bi, 0, ti)),
                 pl.BlockSpec((1, SUBLANES, tb), lambda bi, ti: (bi, 0, ti))]
    return pl.pallas_call(
        functools.partial(_dsaprep_kernel, tb=tb),
        grid=(b, nt),
        in_specs=in_specs,
        out_specs=out_specs,
        out_shape=out_shape,
        compiler_params=_params(("parallel", "parallel")),
        name="dsaprep",
    )(p, p, p, p, *consts)


INT_MIN = np.int32(-2 ** 31)
KEY_NEG_INF = np.int32(np.array(0xFF800000, np.uint32).view(np.int32) ^ np.int32(0x7FFFFFFF))


def _dsa_kernel(kidx_ref, ckv_ref, vt_ref, qidx_ref, qabs_ref, wi_ref, gate_ref, tri_ref,
                o_ref, sc_scr, scb_scr, acc_scr, *, topk):
    i = pl.program_id(1)
    nkb = i // (KB // QB) + 1
    row = lax.broadcasted_iota(I32, (KB, QB), 0)
    col = lax.broadcasted_iota(I32, (KB, QB), 1)
    qpos = i * QB + col
    wi = wi_ref[0]

    def score_body(j, carry):
        lg = _dot(kidx_ref[0, j], qidx_ref[0])
        sc = jnp.zeros((KB, QB), F32)
        for h in range(IDX_HEADS):
            sc = sc + wi[h:h + 1, :] * jnp.maximum(lg[:, h * QB:(h + 1) * QB], 0.0)
        sc = jnp.where(j * KB + row <= qpos, sc, -jnp.inf)
        sc_scr[2 * j] = sc[0:QB]
        sc_scr[2 * j + 1] = sc[QB:KB]
        scb = sc.astype(BF16)
        scb_scr[2 * j] = scb[0:QB]
        scb_scr[2 * j + 1] = scb[QB:KB]
        return carry

    lax.fori_loop(0, nkb, score_body, 0)

    def key_value(key):
        return lax.bitcast_convert_type(jnp.where(key < 0, key ^ np.int32(0x7FFFFFFF), key), F32)

    def search(tail):
        npair = nkb - 1 if tail else nkb

        def count(pred):
            def body(j, accs):
                a0, a1 = accs
                a0 = a0 + _fold8(jnp.where(pred(sc_scr[2 * j]), 1, 0), jnp.add)
                a1 = a1 + _fold8(jnp.where(pred(sc_scr[2 * j + 1]), 1, 0), jnp.add)
                return a0, a1
            zero = jnp.zeros((SUBLANES, QB), I32)
            a0, a1 = lax.fori_loop(0, npair, body, (zero, zero))
            if tail:
                a0 = a0 + _fold8(jnp.where(pred(sc_scr[i]), 1, 0), jnp.add)
            return jnp.sum(a0 + a1, axis=0, keepdims=True)

        def count_bf16(cand):
            one = jnp.ones((QB, QB), BF16)
            zero = jnp.zeros((QB, QB), BF16)

            def hits(c):
                return _fold8(jnp.where(scb_scr[c] >= cand, one, zero), jnp.add,
                              rows=BF16_ROWS).astype(F32)

            def body(j, acc):
                return acc + hits(2 * j) + hits(2 * j + 1)
            acc = lax.fori_loop(0, npair, body, jnp.zeros((BF16_ROWS, QB), F32))
            if tail:
                acc = acc + hits(i)
            return jnp.sum(acc, axis=0, keepdims=True).astype(I32)

        top_key = jnp.full((1, QB), INT_MIN, I32)
        for bit in range(31, 15, -1):
            cand = jnp.zeros((1, QB), I32) if bit == 31 else top_key | np.int32(1 << bit)
            cnt = count_bf16(key_value(cand).astype(BF16))
            top_key = jnp.where(cnt >= topk, cand, top_key)
        bf16_step = np.int32(1 << 16)
        thr_key = jnp.maximum(top_key, INT_MIN + bf16_step) - bf16_step
        hi_key = top_key + (bf16_step - 1)
        for _ in range(17):
            mid = thr_key + ((hi_key - thr_key + 1) >> 1)
            ok = count(lambda s, v=key_value(mid): s >= v) >= topk
            thr_key = jnp.where(ok, mid, thr_key)
            hi_key = jnp.where(ok, hi_key, mid - 1)
        thr = jnp.where(thr_key <= KEY_NEG_INF, -jnp.inf, key_value(thr_key))
        need = (topk - count(lambda s: s > thr)).astype(F32)
        return thr, need

    thr, need = lax.cond(i % 2 == 0, lambda: search(True), lambda: search(False))

    tri = tri_ref[...]

    acc_scr[...] = jnp.zeros_like(acc_scr)

    def attend(c, carry):
        run, ms = carry
        key = sc_scr[c]
        eq = key == thr
        eqf = jnp.where(eq, 1.0, 0.0)
        before = _dot(tri, eqf.astype(BF16)) + run
        sel = (key > thr) | (eq & (before < need))
        sel = sel & (c * QB + lax.broadcasted_iota(I32, (QB, QB), 0) <=
                     i * QB + lax.broadcasted_iota(I32, (QB, QB), 1))
        bias = jnp.where(sel, 0.0, -jnp.inf)
        run = run + _colsum(eqf)
        heads = range(ATTN_HEADS)
        s = _dot(ckv_ref[0, c], qabs_ref[0])
        bias_b = bias.astype(BF16)
        sh = [s[:, h * QB:(h + 1) * QB].astype(BF16) + bias_b for h in heads]
        new_ms = [jnp.maximum(ms[h], _colmax(sh[h], rows=BF16_ROWS).astype(F32)) for h in heads]
        shift = [jnp.where(m == -jnp.inf, 0.0, m) for m in new_ms]
        ps = [jnp.exp2(sh[h] - shift[h].astype(BF16)) for h in heads]
        outs = [_dot(vt_ref[0, c, h * VROWS:(h + 1) * VROWS, :], ps[h]) for h in heads]
        scales = [jnp.broadcast_to(jnp.exp2(ms[h] - shift[h]), (VROWS, QB)) for h in heads]
        acc_scr[...] = acc_scr[...] * jnp.concatenate(scales, axis=0) + \
            jnp.concatenate(outs, axis=0)
        return run, tuple(new_ms)

    ninf = jnp.full((1, QB), -jnp.inf, F32)
    lax.fori_loop(0, i + 1, attend,
                  (jnp.zeros((1, QB), F32), tuple(ninf for _ in range(ATTN_HEADS))))

    o = jnp.concatenate(
        [acc_scr[h * VROWS:h * VROWS + HEAD_DIM, :] *
         (1.0 / acc_scr[h * VROWS + HEAD_DIM:h * VROWS + HEAD_DIM + 1, :])
         for h in range(ATTN_HEADS)], axis=0)
    o = jnp.transpose(o)
    gt = gate_ref[...]
    o_ref[...] = (o * (gt * _sigmoid(gt))).astype(o_ref.dtype)


def _dsa(p, prep, tri, b, t, topk):
    nb = t // QB
    ckv, kidx, vt, qidx, qabs, wit = prep
    nkb = t // KB
    in_specs = [pl.BlockSpec((1, nkb, KB, LANES), lambda bi, qi: (bi, 0, 0, 0)),
                pl.BlockSpec((1, nb, QB, KV_RANK), lambda bi, qi: (bi, 0, 0, 0)),
                pl.BlockSpec((1, nb, ATTN_HEADS * VROWS, QB), lambda bi, qi: (bi, 0, 0, 0)),
                pl.BlockSpec((1, LANES, IDX_HEADS * QB), lambda bi, qi: (bi, 0, qi)),
                pl.BlockSpec((1, KV_RANK, ATTN_HEADS * QB), lambda bi, qi: (bi, 0, qi)),
                pl.BlockSpec((1, SUBLANES, QB), lambda bi, qi: (bi, 0, qi)),
                pl.BlockSpec((QB, ATTN_WIDTH), lambda bi, qi: (bi * nb + qi, C_GA // ATTN_WIDTH)),
                pl.BlockSpec((QB, QB), lambda bi, qi: (0, 0))]
    return pl.pallas_call(
        functools.partial(_dsa_kernel, topk=topk),
        grid=(b, nb),
        in_specs=in_specs,
        out_specs=pl.BlockSpec((QB, ATTN_WIDTH), lambda bi, qi: (bi * nb + qi, 0)),
        out_shape=jax.ShapeDtypeStruct((b * t, ATTN_WIDTH), BF16),
        scratch_shapes=[pltpu.VMEM((nb, QB, QB), F32), pltpu.VMEM((nb, QB, QB), BF16),
                        pltpu.VMEM((ATTN_HEADS * VROWS, QB), F32)],
        compiler_params=_params(("parallel", "arbitrary")),
        name="dsa",
    )(kidx, ckv, vt, qidx, qabs, wit, p, tri)


def _outproj_kernel(yr_ref, ya_ref, x_ref, w_ref, g_ref, o_ref):
    w = w_ref[...]
    y = _dot(yr_ref[...], w[0:RWKV_WIDTH]) + _dot(ya_ref[...], w[RWKV_WIDTH:])
    z = x_ref[...] + y
    ms = jnp.mean(z * z, axis=-1, keepdims=True)
    o_ref[...] = z * lax.rsqrt(ms + NORM_EPS) * g_ref[...]


def _outproj(yr, ya, x2, w_out, final_g, tm):
    n, d = x2.shape
    return pl.pallas_call(
        _outproj_kernel,
        grid=(n // tm,),
        in_specs=[pl.BlockSpec((tm, RWKV_WIDTH), lambda i: (i, 0)),
                  pl.BlockSpec((tm, ATTN_WIDTH), lambda i: (i, 0)),
                  pl.BlockSpec((tm, d), lambda i: (i, 0)),
                  pl.BlockSpec(w_out.shape, lambda i: (0, 0)),
                  pl.BlockSpec((1, d), lambda i: (0, 0))],
        out_specs=pl.BlockSpec((tm, d), lambda i: (i, 0)),
        out_shape=jax.ShapeDtypeStruct((n, d), F32),
        compiler_params=_params(("parallel",)),
        name="outproj",
    )(yr, ya, x2, w_out, final_g)


class _BlockRows(NamedTuple):
    inproj_rows: int
    rwkv_rows: int
    outproj_rows: int
    dsaprep_rows: int


def _block_rows(b, t):
    n = b * t
    assert t % KB == 0 and t % (A_GROUP * CHUNK) == 0
    pick = lambda want, total: want if total % want == 0 else KB
    return _BlockRows(inproj_rows=pick(512, n), rwkv_rows=pick(1024, t),
                      outproj_rows=pick(1024, n), dsaprep_rows=pick(1024, t))


def _pad_rows(w, start, total):
    return jnp.zeros((total, w.shape[1]), w.dtype).at[start:start + w.shape[0]].set(w)


def _layer(x2, b, t, norm_g, w_in, mu_shift, w0, w_up, a0, a_up, k_k, k_a, r_k, gn_g, gn_b,
           q_norm_g, kv_norm_g, w_uq, w_uk, w_uv, w_qidx, kidx_g, kidx_b, w_out, final_g):
    W = RWKV_WIDTH
    d = x2.shape[1]
    o_r, o_k, o_v = 0, W, 2 * W
    o_wd = 3 * W
    o_ad = o_wd + DECAY_LORA
    o_gr = o_ad + AAA_LORA
    o_qd = o_gr + W
    o_kv = o_qd + Q_RANK
    o_ki = o_kv + KV_RANK
    o_wi = o_ki + IDX_DIM
    o_ga = o_wi + IDX_HEADS
    cols = lambda s, n: w_in[:, s:s + n]
    w_all = jnp.concatenate(
        [cols(o_r, 3 * W), cols(o_gr, W), cols(o_ga, ATTN_WIDTH), cols(o_qd, Q_RANK),
         cols(o_kv, KV_RANK), cols(o_wd, DECAY_LORA), cols(o_ad, AAA_LORA), cols(o_ki, IDX_DIM),
         cols(o_wi, IDX_HEADS), jnp.zeros((d, LANES - IDX_HEADS), w_in.dtype)],
        axis=1).astype(BF16)

    row2 = lambda v: v.reshape(1, -1).astype(F32)
    mu_rkv = row2(mu_shift[0:3 * W])
    mu_sm = row2(jnp.concatenate([mu_shift[3 * W:], jnp.zeros((IDX_DIM,), F32)]))
    wup_pad = _pad_rows(w_up, 0, LANES).astype(BF16)
    aup_pad = _pad_rows(a_up, DECAY_LORA, LANES).astype(BF16)
    hid = np.arange(W // 2) // HEAD_DIM
    bd = jnp.asarray((hid[:, None] == hid[None, :]).astype(np.float32), dtype=BF16)

    blk = _block_rows(b, t)
    ti = np.arange(CHUNK)
    tri_r = jnp.asarray((ti[None, :] <= ti[:, None]).astype(np.float32), dtype=BF16)
    rwkv_consts = [mu_rkv, mu_sm, row2(w0), wup_pad, row2(a0), aup_pad, row2(k_k), row2(k_a),
                   row2(r_k), row2(gn_g), row2(gn_b), bd, tri_r]

    wq_t = jnp.transpose(w_qidx.reshape(Q_RANK, IDX_HEADS, IDX_DIM), (1, 2, 0))
    wq_t = jnp.concatenate([jnp.zeros_like(wq_t), wq_t], axis=1).astype(BF16)
    w_uq_heads = jnp.transpose(w_uq.reshape(Q_RANK, ATTN_HEADS, HEAD_DIM), (1, 0, 2))
    wabs_t = _absorb(w_uk, w_uq_heads)
    wuv_t = jnp.transpose(w_uv, (0, 2, 1)).reshape(ATTN_WIDTH, KV_RANK).astype(BF16)
    lng = row2(jnp.concatenate([jnp.zeros((IDX_DIM,), F32), kidx_g]))
    lnb = row2(jnp.concatenate([jnp.zeros((IDX_DIM,), F32), kidx_b]))
    prep_consts = [row2(q_norm_g), row2(kv_norm_g), lng, lnb, wq_t, wabs_t, wuv_t]
    ki = np.arange(QB)
    tri_q = jnp.asarray((ki[None, :] < ki[:, None]).astype(np.float32), dtype=BF16)

    p = _inproj(x2, row2(norm_g), w_all, blk.inproj_rows)
    y_r = _rwkv(p, b, t, blk.rwkv_rows, rwkv_consts)
    prep = _dsaprep(p, b, t, blk.dsaprep_rows, prep_consts)
    y_a = _dsa(p, prep, tri_q, b, t, min(TOPK_MAX, t // 4))
    return _outproj(y_r, y_a, x2, w_out.astype(BF16), row2(final_g), blk.outproj_rows)


def kernel(x, norm_g, w_in, mu_shift, w0, w_up, a0, a_up, k_k, k_a, r_k, gn_g, gn_b,
           q_norm_g, kv_norm_g, w_uq, w_uk, w_uv, w_qidx, kidx_g, kidx_b, w_out, final_g):
    b, t, d = x.shape
    assert norm_g.shape[0] == 1, "single-layer problem"
    out = _layer(x.reshape(b * t, d), b, t, norm_g[0], w_in[0], mu_shift[0], w0[0], w_up[0],
                 a0[0], a_up[0], k_k[0], k_a[0], r_k[0], gn_g[0], gn_b[0], q_norm_g[0],
                 kv_norm_g[0], w_uq[0], w_uk[0], w_uv[0], w_qidx[0], kidx_g[0], kidx_b[0],
                 w_out[0], final_g)
    return out.reshape(b, t, d)
```

```python
import functools
from typing import NamedTuple

import numpy as np
import jax
import jax.numpy as jnp
from jax import lax
from jax.experimental import pallas as pl
from jax.experimental.pallas import tpu as pltpu

F32 = jnp.float32
BF16 = jnp.bfloat16
I32 = jnp.int32

HEAD_DIM = 64
RWKV_HEADS = 8
RWKV_WIDTH = RWKV_HEADS * HEAD_DIM
DECAY_LORA = 32
AAA_LORA = 32
ATTN_HEADS = 8
ATTN_WIDTH = ATTN_HEADS * HEAD_DIM
Q_RANK = 256
KV_RANK = 128
IDX_HEADS = 4
IDX_DIM = 64
TOPK_MAX = 256
NORM_EPS = 1e-6
LN_EPS = 1e-5
GN_EPS = 64e-5

LANES = 128
SUBLANES = 8
BF16_ROWS = 16
QB = 256
KB = 2 * QB
CHUNK = 64
PAIR = 2 * HEAD_DIM
A_GROUP = 2
VROWS = HEAD_DIM + BF16_ROWS

C_RKV = 0
C_GR = 3 * RWKV_WIDTH
C_GA = C_GR + RWKV_WIDTH
C_QD = C_GA + ATTN_WIDTH
C_KV = C_QD + Q_RANK
C_SM = C_KV + KV_RANK
C_WI = C_SM + LANES
P_COLS = C_WI + LANES

VMEM_LIMIT = 56 * 1024 * 1024


def _params(sem):
    return pltpu.CompilerParams(dimension_semantics=sem, vmem_limit_bytes=VMEM_LIMIT)


def _dot(a, b):
    return jnp.dot(a, b, preferred_element_type=F32)


def _dot_nt(a, b):
    return lax.dot_general(a, b, (((1,), (1,)), ((), ())), preferred_element_type=F32)


def _dot_tn(a, b):
    return lax.dot_general(a, b, (((0,), (0,)), ((), ())), preferred_element_type=F32)


def _split2(x):
    hi = x.astype(BF16)
    lo = (x - hi.astype(F32)).astype(BF16)
    return hi, lo


def _fold8(x, op, rows=SUBLANES):
    acc = x[0:rows]
    for j in range(1, x.shape[0] // rows):
        acc = op(acc, x[j * rows:(j + 1) * rows])
    return acc


def _colsum(x):
    return jnp.sum(_fold8(x, jnp.add), axis=0, keepdims=True)


def _colmax(x, rows=SUBLANES):
    return jnp.max(_fold8(x, jnp.maximum, rows), axis=0, keepdims=True)


def _sigmoid(x):
    return 1.0 / (1.0 + jnp.exp(-x))


def _absorb_kernel(wuk_ref, wuq_ref, o_ref):
    a = wuk_ref[0]
    b = wuq_ref[0]
    ah, al = _split2(a)
    bh, bl = _split2(b)
    acc = _dot_nt(ah, bh) + _dot_nt(ah, bl) + _dot_nt(al, bh)
    o_ref[0] = acc * np.float32(HEAD_DIM ** -0.5 * np.log2(np.e))


def _absorb(w_uk, w_uq_heads):
    return pl.pallas_call(
        _absorb_kernel,
        grid=(ATTN_HEADS,),
        in_specs=[pl.BlockSpec((1, KV_RANK, HEAD_DIM), lambda h: (h, 0, 0)),
                  pl.BlockSpec((1, Q_RANK, HEAD_DIM), lambda h: (h, 0, 0))],
        out_specs=pl.BlockSpec((1, KV_RANK, Q_RANK), lambda h: (h, 0, 0)),
        out_shape=jax.ShapeDtypeStruct((ATTN_HEADS, KV_RANK, Q_RANK), F32),
        compiler_params=_params(("arbitrary",)),
        name="absorb",
    )(w_uk, w_uq_heads)


def _inproj_kernel(x_ref, g_ref, w_ref, o_ref):
    x = x_ref[...]
    ms = jnp.mean(x * x, axis=-1, keepdims=True)
    xn = x * lax.rsqrt(ms + NORM_EPS) * g_ref[...]
    o_ref[...] = _dot(xn.astype(BF16), w_ref[...])


def _inproj(x2, norm_g, w_all, tm):
    n, d = x2.shape
    return pl.pallas_call(
        _inproj_kernel,
        grid=(n // tm,),
        in_specs=[pl.BlockSpec((tm, d), lambda i: (i, 0)),
                  pl.BlockSpec((1, d), lambda i: (0, 0)),
                  pl.BlockSpec((d, P_COLS), lambda i: (0, 0))],
        out_specs=pl.BlockSpec((tm, P_COLS), lambda i: (i, 0)),
        out_shape=jax.ShapeDtypeStruct((n, P_COLS), F32),
        compiler_params=_params(("parallel",)),
        name="inproj",
    )(x2, norm_g, w_all)


def _rwkv_kernel(rkv_ref, sm_ref, gate_ref, mu_rkv_ref, mu_sm_ref, w0_ref, wup_ref, a0_ref,
                 aup_ref, kkw_ref, ka_ref, rk_ref, gng_ref, gnb_ref, bd_ref, tri_ref,
                 o_ref,
                 carry_rkv, carry_sm, s_scr, n_scr, r_scr, u_scr, k_scr, v_scr, wc_scr, bv_scr,
                 t_scr, ankv_scr, ar_scr, *, tb):
    t = pl.program_id(1)
    W = RWKV_WIDTH

    @pl.when(t == 0)
    def _():
        carry_rkv[...] = jnp.zeros_like(carry_rkv)
        carry_sm[...] = jnp.zeros_like(carry_sm)
        s_scr[...] = jnp.zeros_like(s_scr)

    bd = bd_ref[...]
    half = W // 2

    def head_sum(x):
        st = jnp.concatenate([x[:, :half], x[:, half:]], axis=0)
        res = _dot(st.astype(BF16), bd)
        return jnp.concatenate([res[:CHUNK], res[CHUNK:]], axis=1)

    def shifted(ref, carry, c, rows):
        cur = ref[rows, :]
        first = carry[0:1, :] if c == 0 else ref[c * CHUNK - 1:c * CHUNK, :]
        row = lax.broadcasted_iota(I32, (CHUNK, 1), 0)
        return cur, jnp.where(row == 0, first, pltpu.roll(cur, 1, 0))

    def prep(c):
        rows = pl.ds(c * CHUNK, CHUNK)
        p, prev = shifted(rkv_ref, carry_rkv, c, rows)
        xs = p + (prev - p) * mu_rkv_ref[...]
        r = xs[:, 0:W]
        k = xs[:, W:2 * W]
        v = xs[:, 2 * W:3 * W]
        ps, prev_s = shifted(sm_ref, carry_sm, c, rows)
        xl = ps + (prev_s - ps) * mu_sm_ref[...]
        zw = w0_ref[...] + _dot(jnp.tanh(xl).astype(BF16), wup_ref[...])
        za = a0_ref[...] + _dot(xl.astype(BF16), aup_ref[...])
        yield
        logw = -np.float32(np.exp(-0.5)) * _sigmoid(zw)
        a = _sigmoid(za)
        kk = k * kkw_ref[...]
        ss = head_sum(kk * kk)
        hi, lo = _split2(logw)
        tri = tri_ref[...]
        cum = _dot(tri, hi) + _dot(tri, lo)
        yield
        kk = kk * (1.0 / jnp.maximum(jnp.sqrt(ss), 1e-12))
        k2 = k * (1.0 + (a - 1.0) * ka_ref[...])
        bonus = head_sum(r * k2 * rk_ref[...])
        wt = jnp.exp(cum)
        wi = jnp.exp(-cum)
        wx = jnp.exp(cum - logw)
        n_scr[rows, :] = (-kk * wx).astype(BF16)
        r_scr[rows, :] = (r * wt).astype(BF16)
        u_scr[rows, :] = (kk * a * wi).astype(BF16)
        k_scr[rows, :] = (k2 * wi).astype(BF16)
        v_scr[rows, :] = v.astype(BF16)
        wc_scr[c] = wt[CHUNK - SUBLANES:CHUNK, :]
        yield
        bv_scr[rows, :] = bonus * v

    rowp = lax.broadcasted_iota(I32, (PAIR, PAIR), 0)
    colp = lax.broadcasted_iota(I32, (PAIR, PAIR), 1)
    same_blk = (rowp >= HEAD_DIM) == (colp >= HEAD_DIM)
    strict_lo = (colp % HEAD_DIM) < (rowp % HEAD_DIM)
    m_abd = same_blk & strict_lo
    m_ank = jnp.logical_not(same_blk) & strict_lo
    rowc = lax.broadcasted_iota(I32, (CHUNK, PAIR), 0)
    colc = lax.broadcasted_iota(I32, (CHUNK, PAIR), 1)
    incl_lo2 = (lax.broadcasted_iota(I32, (CHUNK, 2 * PAIR), 1) % HEAD_DIM) <= \
        lax.broadcasted_iota(I32, (CHUNK, 2 * PAIR), 0)
    head0_c = colc < HEAD_DIM
    head0_p = colp < HEAD_DIM

    pairs = range(RWKV_HEADS // 2)
    lanes = [slice(g * PAIR, (g + 1) * PAIR) for g in pairs]
    eye = (rowp == colp).astype(F32)

    def phase_a(chunks):
        chains = [(c, g) for c in chunks for g in pairs]
        gtop = []
        for c, g in chains:
            rows = pl.ds(c * CHUNK, CHUNK)
            sl = lanes[g]
            nr = jnp.concatenate([n_scr[rows, sl], r_scr[rows, sl]], axis=0)
            uu = u_scr[rows, sl]
            kc = k_scr[rows, sl]
            zero = jnp.zeros_like(nr)
            rhs = jnp.concatenate(
                [jnp.where(head0_p, jnp.concatenate([uu, kc], axis=0), zero),
                 jnp.where(head0_p, zero, jnp.concatenate([kc, uu], axis=0))], axis=0)
            gg = _dot_nt(nr, rhs)
            ar_scr[c, g] = jnp.where(incl_lo2, gg[CHUNK:PAIR], 0.0).astype(BF16)
            gtop.append(jnp.concatenate([gg[0:CHUNK, 0:PAIR], gg[0:CHUNK, PAIR:]], axis=0))
        yield
        for i, (c, g) in enumerate(chains):
            vv = v_scr[pl.ds(c * CHUNK, CHUNK), lanes[g]]
            ank = jnp.where(m_ank, gtop[i], 0.0).astype(BF16)
            ankv_scr[c, g] = _dot(ank, jnp.concatenate([vv, vv], axis=0))
        n = range(len(chains))
        pw = [jnp.where(m_abd, gtop[i], 0.0) for i in n]
        tm = [eye + pw[i] for i in n]
        pwb = [x.astype(BF16) for x in pw]
        pw = [_dot(x, x) for x in pwb]
        yield
        for _ in range(4):
            pwb = [x.astype(BF16) for x in pw]
            both = [_dot(pwb[i], jnp.concatenate([pwb[i], tm[i].astype(BF16)], axis=1))
                    for i in n]
            pw = [x[:, 0:PAIR] for x in both]
            tm = [tm[i] + both[i][:, PAIR:] for i in n]
            yield
        tm = [tm[i] + _dot(pw[i].astype(BF16), tm[i].astype(BF16)) for i in n]
        for i, (c, g) in enumerate(chains):
            t_scr[c, g] = tm[i].astype(BF16)

    def phase_b(c):
        rows = pl.ds(c * CHUNK, CHUNK)
        wc_all = wc_scr[c][SUBLANES - 1:SUBLANES, :]
        s_old = [s_scr[g] for g in pairs]
        nrh = [_dot_nt(jnp.concatenate([n_scr[rows, lanes[g]], r_scr[rows, lanes[g]]], axis=0),
                       s_old[g].astype(BF16)) for g in pairs]
        yield
        zf = []
        for g in pairs:
            nh = nrh[g][0:CHUNK]
            bf = jnp.concatenate([nh, nh], axis=0) + ankv_scr[c, g]
            zf.append(_dot(t_scr[c, g], bf.astype(BF16)))
        yield
        ys = []
        for g in pairs:
            sl = lanes[g]
            vv = v_scr[rows, sl]
            zb = jnp.where(head0_c, zf[g][0:CHUNK], zf[g][CHUNK:PAIR]).astype(BF16)
            zv = jnp.concatenate([zb, vv], axis=0)
            vz = jnp.concatenate([vv, zb], axis=0)
            zero = jnp.zeros_like(zv)
            rhs = jnp.concatenate([jnp.where(head0_p, zv, zero), jnp.where(head0_p, zero, vz)],
                                  axis=0)
            ys.append(nrh[g][CHUNK:PAIR] + _dot(ar_scr[c, g], rhs))
            uk = jnp.concatenate([u_scr[rows, sl], k_scr[rows, sl]], axis=0)
            s_new = (s_old[g] + _dot_tn(zv, uk)) * wc_all[:, sl]
            s_scr[g] = jnp.where(same_blk, s_new, 0.0)
        yield
        y = jnp.concatenate(ys, axis=1)
        inv_d = np.float32(1.0 / HEAD_DIM)
        yc = y - head_sum(y) * inv_d
        yield
        var = head_sum(yc * yc) * inv_d
        y = yc * lax.rsqrt(var + GN_EPS) * gng_ref[...] + gnb_ref[...] + bv_scr[rows, :]
        gt = gate_ref[rows, :]
        o_ref[rows, :] = (y * (gt * _sigmoid(gt))).astype(o_ref.dtype)

    def in_step(gens):
        gens = list(gens)
        while gens:
            gens = [g for g in gens if next(g, True) is None]
            if gens:
                yield

    def in_turn(gens):
        for g in gens:
            yield from g

    nchunks = tb // CHUNK
    groups = [list(range(c0, c0 + A_GROUP)) for c0 in range(0, nchunks, A_GROUP)]
    for s in range(len(groups) + 2):
        streams = []
        if s - 2 >= 0:
            streams.append(in_turn(phase_b(c) for c in groups[s - 2]))
        if 0 <= s - 1 < len(groups):
            streams.append(phase_a(groups[s - 1]))
        if s < len(groups):
            streams.append(in_step(prep(c) for c in groups[s]))
        while streams:
            streams = [g for g in streams if next(g, True) is None]
    carry_rkv[0:1, :] = rkv_ref[tb - 1:tb, :]
    carry_sm[0:1, :] = sm_ref[tb - 1:tb, :]


def _rwkv(p, b, t, tb, consts):
    nt = t // tb
    W = RWKV_WIDTH
    row_spec = lambda width, cb: pl.BlockSpec((tb, width), lambda bi, ti: (bi * nt + ti, cb))
    full = lambda arr: pl.BlockSpec(arr.shape, lambda bi, ti: (0,) * arr.ndim)
    in_specs = [row_spec(3 * W, C_RKV // (3 * W)), row_spec(LANES, C_SM // LANES),
                row_spec(W, C_GR // W)] + [full(c) for c in consts]
    return pl.pallas_call(
        functools.partial(_rwkv_kernel, tb=tb),
        grid=(b, nt),
        in_specs=in_specs,
        out_specs=pl.BlockSpec((tb, W), lambda bi, ti: (bi * nt + ti, 0)),
        out_shape=jax.ShapeDtypeStruct((b * t, W), BF16),
        scratch_shapes=[pltpu.VMEM((SUBLANES, 3 * W), F32), pltpu.VMEM((SUBLANES, LANES), F32),
                        pltpu.VMEM((RWKV_HEADS // 2, PAIR, PAIR), F32),
                        pltpu.VMEM((tb, W), BF16), pltpu.VMEM((tb, W), BF16),
                        pltpu.VMEM((tb, W), BF16), pltpu.VMEM((tb, W), BF16),
                        pltpu.VMEM((tb, W), BF16),
                        pltpu.VMEM((tb // CHUNK, SUBLANES, W), F32),
                        pltpu.VMEM((tb, W), F32),
                        pltpu.VMEM((tb // CHUNK, RWKV_HEADS // 2, PAIR, PAIR), BF16),
                        pltpu.VMEM((tb // CHUNK, RWKV_HEADS // 2, PAIR, PAIR), F32),
                        pltpu.VMEM((tb // CHUNK, RWKV_HEADS // 2, CHUNK, 2 * PAIR), BF16)],
        compiler_params=_params(("parallel", "arbitrary")),
        name="rwkv",
    )(p, p, p, *consts)


def _dsaprep_kernel(qd_ref, kv_ref, sm_ref, wi_ref, qg_ref, kvg_ref, lng_ref, lnb_ref,
                    wqidx_ref, wabs_ref, wuv_ref,
                    ckv_ref, kidx_ref, vt_ref, qidx_ref, qabs_ref, wit_ref, *, tb):
    nq = tb // QB
    qd = qd_ref[...]
    cq = qd * lax.rsqrt(jnp.mean(qd * qd, axis=-1, keepdims=True) + NORM_EPS) * qg_ref[...]
    cqb = cq.astype(BF16)
    kv = kv_ref[...]
    ckv = kv * lax.rsqrt(jnp.mean(kv * kv, axis=-1, keepdims=True) + NORM_EPS) * kvg_ref[...]
    ckvb = ckv.astype(BF16)

    sm = sm_ref[...]
    lane = lax.broadcasted_iota(I32, sm.shape, 1)
    hi_half = lane >= IDX_DIM
    inv_d = np.float32(1.0 / IDX_DIM)
    mu = jnp.sum(jnp.where(hi_half, sm, 0.0), axis=-1, keepdims=True) * inv_d
    xc = jnp.where(hi_half, sm - mu, 0.0)
    var = jnp.sum(xc * xc, axis=-1, keepdims=True) * inv_d
    kidx = (xc * lax.rsqrt(var + LN_EPS) * lng_ref[...] + lnb_ref[...]).astype(BF16)

    vt = _dot_nt(wuv_ref[...], ckvb).astype(BF16)
    ones = jnp.ones((VROWS - HEAD_DIM, tb), BF16)
    vt = jnp.concatenate(
        [x for h in range(ATTN_HEADS) for x in (vt[h * HEAD_DIM:(h + 1) * HEAD_DIM], ones)],
        axis=0)
    wit = jnp.transpose(wi_ref[...])[0:SUBLANES, :] * np.float32(IDX_HEADS ** -0.5)
    for j in range(tb // KB):
        kidx_ref[0, j] = kidx[j * KB:(j + 1) * KB, :]
    for j in range(nq):
        ckv_ref[0, j] = ckvb[j * QB:(j + 1) * QB, :]
        vt_ref[0, j] = vt[:, j * QB:(j + 1) * QB]
    wit_ref[0] = wit
    for h in range(IDX_HEADS):
        res = _dot_nt(wqidx_ref[h], cqb).astype(BF16)
        for j in range(nq):
            qidx_ref[0, :, (j * IDX_HEADS + h) * QB:(j * IDX_HEADS + h + 1) * QB] = \
                res[:, j * QB:(j + 1) * QB]
    for h in range(ATTN_HEADS):
        res = _dot_nt(wabs_ref[h].astype(BF16), cqb).astype(BF16)
        for j in range(nq):
            qabs_ref[0, :, (j * ATTN_HEADS + h) * QB:(j * ATTN_HEADS + h + 1) * QB] = \
                res[:, j * QB:(j + 1) * QB]


def _dsaprep(p, b, t, tb, consts):
    nt = t // tb
    nb = t // QB
    nq = tb // QB
    row_spec = lambda width, cb: pl.BlockSpec((tb, width), lambda bi, ti: (bi * nt + ti, cb))
    full = lambda arr: pl.BlockSpec(arr.shape, lambda bi, ti: (0,) * arr.ndim)
    in_specs = [row_spec(Q_RANK, C_QD // Q_RANK), row_spec(KV_RANK, C_KV // KV_RANK),
                row_spec(LANES, C_SM // LANES), row_spec(LANES, C_WI // LANES)] + \
               [full(c) for c in consts]
    nkb = t // KB
    nk = tb // KB
    out_shape = [jax.ShapeDtypeStruct((b, nb, QB, KV_RANK), BF16),
                 jax.ShapeDtypeStruct((b, nkb, KB, LANES), BF16),
                 jax.ShapeDtypeStruct((b, nb, ATTN_HEADS * VROWS, QB), BF16),
                 jax.ShapeDtypeStruct((b, LANES, nb * IDX_HEADS * QB), BF16),
                 jax.ShapeDtypeStruct((b, KV_RANK, nb * ATTN_HEADS * QB), BF16),
                 jax.ShapeDtypeStruct((b, SUBLANES, t), F32)]
    out_specs = [pl.BlockSpec((1, nq, QB, KV_RANK), lambda bi, ti: (bi, ti, 0, 0)),
                 pl.BlockSpec((1, nk, KB, LANES), lambda bi, ti: (bi, ti, 0, 0)),
                 pl.BlockSpec((1, nq, ATTN_HEADS * VROWS, QB), lambda bi, ti: (bi, ti, 0, 0)),
                 pl.BlockSpec((1, LANES, nq * IDX_HEADS * QB), lambda bi, ti: (bi, 0, ti)),
                 pl.BlockSpec((1, KV_RANK, nq * ATTN_HEADS * QB), lambda bi, ti: (bi, 0, ti)),
                 pl.BlockSpec((1, SUBLANES, tb), lambda bi, ti: (bi, 0, ti))]
    return pl.pallas_call(
        functools.partial(_dsaprep_kernel, tb=tb),
        grid=(b, nt),
        in_specs=in_specs,
        out_specs=out_specs,
        out_shape=out_shape,
        compiler_params=_params(("parallel", "parallel")),
        name="dsaprep",
    )(p, p, p, p, *consts)


INT_MIN = np.int32(-2 ** 31)
KEY_NEG_INF = np.int32(np.array(0xFF800000, np.uint32).view(np.int32) ^ np.int32(0x7FFFFFFF))


def _dsa_kernel(kidx_ref, ckv_ref, vt_ref, qidx_ref, qabs_ref, wi_ref, gate_ref, tri_ref,
                o_ref, sc_scr, scb_scr, acc_scr, *, topk):
    i = pl.program_id(1)
    nkb = i // (KB // QB) + 1
    wi = wi_ref[0]

    def scores(kidx, key0):
        lg = _dot(kidx, qidx_ref[0])
        n = kidx.shape[0]
        sc = jnp.zeros((n, QB), F32)
        for h in range(IDX_HEADS):
            sc = sc + wi[h:h + 1, :] * jnp.maximum(lg[:, h * QB:(h + 1) * QB], 0.0)
        causal = key0 + lax.broadcasted_iota(I32, (n, QB), 0) <= \
            i * QB + lax.broadcasted_iota(I32, (n, QB), 1)
        return jnp.where(causal, sc, -jnp.inf)

    def score_body(j, carry):
        sc = scores(kidx_ref[0, j], j * KB)
        sc_scr[2 * j] = sc[0:QB]
        sc_scr[2 * j + 1] = sc[QB:KB]
        scb = sc.astype(BF16)
        scb_scr[2 * j] = scb[0:QB]
        scb_scr[2 * j + 1] = scb[QB:KB]
        return carry

    lax.fori_loop(0, (i + 1) // 2, score_body, 0)

    def key_value(key):
        return lax.bitcast_convert_type(jnp.where(key < 0, key ^ np.int32(0x7FFFFFFF), key), F32)

    def search(tail):
        npair = nkb - 1 if tail else nkb
        if tail:
            sc = scores(kidx_ref[0, nkb - 1, 0:QB, :], i * QB)
            sc_scr[i] = sc
            scb_scr[i] = sc.astype(BF16)

        def count(pred):
            def body(j, accs):
                a0, a1 = accs
                a0 = a0 + _fold8(jnp.where(pred(sc_scr[2 * j]), 1, 0), jnp.add)
                a1 = a1 + _fold8(jnp.where(pred(sc_scr[2 * j + 1]), 1, 0), jnp.add)
                return a0, a1
            zero = jnp.zeros((SUBLANES, QB), I32)
            a0, a1 = lax.fori_loop(0, npair, body, (zero, zero))
            if tail:
                a0 = a0 + _fold8(jnp.where(pred(sc_scr[i]), 1, 0), jnp.add)
            return jnp.sum(a0 + a1, axis=0, keepdims=True)

        def count_bf16(cand):
            one = jnp.ones((QB, QB), BF16)
            zero = jnp.zeros((QB, QB), BF16)

            def hits(c):
                return _fold8(jnp.where(scb_scr[c] >= cand, one, zero), jnp.add,
                              rows=BF16_ROWS).astype(F32)

            def body(j, acc):
                return acc + hits(2 * j) + hits(2 * j + 1)
            acc = lax.fori_loop(0, npair, body, jnp.zeros((BF16_ROWS, QB), F32))
            if tail:
                acc = acc + hits(i)
            return jnp.sum(acc, axis=0, keepdims=True).astype(I32)

        top_key = jnp.full((1, QB), INT_MIN, I32)
        for bit in range(31, 15, -1):
            cand = jnp.zeros((1, QB), I32) if bit == 31 else top_key | np.int32(1 << bit)
            cnt = count_bf16(key_value(cand).astype(BF16))
            top_key = jnp.where(cnt >= topk, cand, top_key)
        bf16_step = np.int32(1 << 16)
        thr_key = jnp.maximum(top_key, INT_MIN + bf16_step) - bf16_step
        hi_key = top_key + (bf16_step - 1)
        for _ in range(17):
            mid = thr_key + ((hi_key - thr_key + 1) >> 1)
            ok = count(lambda s, v=key_value(mid): s >= v) >= topk
            thr_key = jnp.where(ok, mid, thr_key)
            hi_key = jnp.where(ok, hi_key, mid - 1)
        thr = jnp.where(thr_key <= KEY_NEG_INF, -jnp.inf, key_value(thr_key))
        need = (topk - count(lambda s: s > thr)).astype(F32)
        return thr, need

    thr, need = lax.cond(i % 2 == 0, lambda: search(True), lambda: search(False))

    tri = tri_ref[...]

    acc_scr[...] = jnp.zeros_like(acc_scr)

    def attend(c, carry):
        run, ms = carry
        key = sc_scr[c]
        eq = key == thr
        eqf = jnp.where(eq, 1.0, 0.0)
        before = _dot(tri, eqf.astype(BF16)) + run
        sel = (key > thr) | (eq & (before < need))
        sel = sel & (c * QB + lax.broadcasted_iota(I32, (QB, QB), 0) <=
                     i * QB + lax.broadcasted_iota(I32, (QB, QB), 1))
        bias = jnp.where(sel, 0.0, -jnp.inf)
        run = run + _colsum(eqf)
        heads = range(ATTN_HEADS)
        s = _dot(ckv_ref[0, c], qabs_ref[0])
        bias_b = bias.astype(BF16)
        sh = [s[:, h * QB:(h + 1) * QB].astype(BF16) + bias_b for h in heads]
        new_ms = [jnp.maximum(ms[h], _colmax(sh[h], rows=BF16_ROWS).astype(F32)) for h in heads]
        shift = [jnp.where(m == -jnp.inf, 0.0, m) for m in new_ms]
        ps = [jnp.exp2(sh[h] - shift[h].astype(BF16)) for h in heads]
        outs = [_dot(vt_ref[0, c, h * VROWS:(h + 1) * VROWS, :], ps[h]) for h in heads]
        scales = [jnp.broadcast_to(jnp.exp2(ms[h] - shift[h]), (VROWS, QB)) for h in heads]
        acc_scr[...] = acc_scr[...] * jnp.concatenate(scales, axis=0) + \
            jnp.concatenate(outs, axis=0)
        return run, tuple(new_ms)

    ninf = jnp.full((1, QB), -jnp.inf, F32)
    lax.fori_loop(0, i + 1, attend,
                  (jnp.zeros((1, QB), F32), tuple(ninf for _ in range(ATTN_HEADS))))

    o = jnp.concatenate(
        [acc_scr[h * VROWS:h * VROWS + HEAD_DIM, :] *
         (1.0 / acc_scr[h * VROWS + HEAD_DIM:h * VROWS + HEAD_DIM + 1, :])
         for h in range(ATTN_HEADS)], axis=0)
    o = jnp.transpose(o)
    gt = gate_ref[...]
    o_ref[...] = (o * (gt * _sigmoid(gt))).astype(o_ref.dtype)


def _dsa(p, prep, tri, b, t, topk):
    nb = t // QB
    ckv, kidx, vt, qidx, qabs, wit = prep
    nkb = t // KB
    in_specs = [pl.BlockSpec((1, nkb, KB, LANES), lambda bi, qi: (bi, 0, 0, 0)),
                pl.BlockSpec((1, nb, QB, KV_RANK), lambda bi, qi: (bi, 0, 0, 0)),
                pl.BlockSpec((1, nb, ATTN_HEADS * VROWS, QB), lambda bi, qi: (bi, 0, 0, 0)),
                pl.BlockSpec((1, LANES, IDX_HEADS * QB), lambda bi, qi: (bi, 0, qi)),
                pl.BlockSpec((1, KV_RANK, ATTN_HEADS * QB), lambda bi, qi: (bi, 0, qi)),
                pl.BlockSpec((1, SUBLANES, QB), lambda bi, qi: (bi, 0, qi)),
                pl.BlockSpec((QB, ATTN_WIDTH), lambda bi, qi: (bi * nb + qi, C_GA // ATTN_WIDTH)),
                pl.BlockSpec((QB, QB), lambda bi, qi: (0, 0))]
    return pl.pallas_call(
        functools.partial(_dsa_kernel, topk=topk),
        grid=(b, nb),
        in_specs=in_specs,
        out_specs=pl.BlockSpec((QB, ATTN_WIDTH), lambda bi, qi: (bi * nb + qi, 0)),
        out_shape=jax.ShapeDtypeStruct((b * t, ATTN_WIDTH), BF16),
        scratch_shapes=[pltpu.VMEM((nb, QB, QB), F32), pltpu.VMEM((nb, QB, QB), BF16),
                        pltpu.VMEM((ATTN_HEADS * VROWS, QB), F32)],
        compiler_params=_params(("parallel", "arbitrary")),
        name="dsa",
    )(kidx, ckv, vt, qidx, qabs, wit, p, tri)


def _outproj_kernel(yr_ref, ya_ref, x_ref, w_ref, g_ref, o_ref):
    w = w_ref[...]
    y = _dot(yr_ref[...], w[0:RWKV_WIDTH]) + _dot(ya_ref[...], w[RWKV_WIDTH:])
    z = x_ref[...] + y
    ms = jnp.mean(z * z, axis=-1, keepdims=True)
    o_ref[...] = z * lax.rsqrt(ms + NORM_EPS) * g_ref[...]


def _outproj(yr, ya, x2, w_out, final_g, tm):
    n, d = x2.shape
    return pl.pallas_call(
        _outproj_kernel,
        grid=(n // tm,),
        in_specs=[pl.BlockSpec((tm, RWKV_WIDTH), lambda i: (i, 0)),
                  pl.BlockSpec((tm, ATTN_WIDTH), lambda i: (i, 0)),
                  pl.BlockSpec((tm, d), lambda i: (i, 0)),
                  pl.BlockSpec(w_out.shape, lambda i: (0, 0)),
                  pl.BlockSpec((1, d), lambda i: (0, 0))],
        out_specs=pl.BlockSpec((tm, d), lambda i: (i, 0)),
        out_shape=jax.ShapeDtypeStruct((n, d), F32),
        compiler_params=_params(("parallel",)),
        name="outproj",
    )(yr, ya, x2, w_out, final_g)


class _BlockRows(NamedTuple):
    inproj_rows: int
    rwkv_rows: int
    outproj_rows: int
    dsaprep_rows: int


def _block_rows(b, t):
    n = b * t
    assert t % KB == 0 and t % (A_GROUP * CHUNK) == 0
    pick = lambda want, total: want if total % want == 0 else KB
    return _BlockRows(inproj_rows=pick(1024, n), rwkv_rows=pick(1024, t),
                      outproj_rows=pick(1024, n), dsaprep_rows=pick(1024, t))


def _pad_rows(w, start, total):
    return jnp.zeros((total, w.shape[1]), w.dtype).at[start:start + w.shape[0]].set(w)


def _layer(x2, b, t, norm_g, w_in, mu_shift, w0, w_up, a0, a_up, k_k, k_a, r_k, gn_g, gn_b,
           q_norm_g, kv_norm_g, w_uq, w_uk, w_uv, w_qidx, kidx_g, kidx_b, w_out, final_g):
    W = RWKV_WIDTH
    d = x2.shape[1]
    o_r, o_k, o_v = 0, W, 2 * W
    o_wd = 3 * W
    o_ad = o_wd + DECAY_LORA
    o_gr = o_ad + AAA_LORA
    o_qd = o_gr + W
    o_kv = o_qd + Q_RANK
    o_ki = o_kv + KV_RANK
    o_wi = o_ki + IDX_DIM
    o_ga = o_wi + IDX_HEADS
    cols = lambda s, n: w_in[:, s:s + n]
    w_all = jnp.concatenate(
        [cols(o_r, 3 * W), cols(o_gr, W), cols(o_ga, ATTN_WIDTH), cols(o_qd, Q_RANK),
         cols(o_kv, KV_RANK), cols(o_wd, DECAY_LORA), cols(o_ad, AAA_LORA), cols(o_ki, IDX_DIM),
         cols(o_wi, IDX_HEADS), jnp.zeros((d, LANES - IDX_HEADS), w_in.dtype)],
        axis=1).astype(BF16)

    row2 = lambda v: v.reshape(1, -1).astype(F32)
    mu_rkv = row2(mu_shift[0:3 * W])
    mu_sm = row2(jnp.concatenate([mu_shift[3 * W:], jnp.zeros((IDX_DIM,), F32)]))
    wup_pad = _pad_rows(w_up, 0, LANES).astype(BF16)
    aup_pad = _pad_rows(a_up, DECAY_LORA, LANES).astype(BF16)
    hid = np.arange(W // 2) // HEAD_DIM
    bd = jnp.asarray((hid[:, None] == hid[None, :]).astype(np.float32), dtype=BF16)

    blk = _block_rows(b, t)
    ti = np.arange(CHUNK)
    tri_r = jnp.asarray((ti[None, :] <= ti[:, None]).astype(np.float32), dtype=BF16)
    rwkv_consts = [mu_rkv, mu_sm, row2(w0), wup_pad, row2(a0), aup_pad, row2(k_k), row2(k_a),
                   row2(r_k), row2(gn_g), row2(gn_b), bd, tri_r]

    wq_t = jnp.transpose(w_qidx.reshape(Q_RANK, IDX_HEADS, IDX_DIM), (1, 2, 0))
    wq_t = jnp.concatenate([jnp.zeros_like(wq_t), wq_t], axis=1).astype(BF16)
    w_uq_heads = jnp.transpose(w_uq.reshape(Q_RANK, ATTN_HEADS, HEAD_DIM), (1, 0, 2))
    wabs_t = _absorb(w_uk, w_uq_heads)
    wuv_t = jnp.transpose(w_uv, (0, 2, 1)).reshape(ATTN_WIDTH, KV_RANK).astype(BF16)
    lng = row2(jnp.concatenate([jnp.zeros((IDX_DIM,), F32), kidx_g]))
    lnb = row2(jnp.concatenate([jnp.zeros((IDX_DIM,), F32), kidx_b]))
    prep_consts = [row2(q_norm_g), row2(kv_norm_g), lng, lnb, wq_t, wabs_t, wuv_t]
    ki = np.arange(QB)
    tri_q = jnp.asarray((ki[None, :] < ki[:, None]).astype(np.float32), dtype=BF16)

    p = _inproj(x2, row2(norm_g), w_all, blk.inproj_rows)
    y_r = _rwkv(p, b, t, blk.rwkv_rows, rwkv_consts)
    prep = _dsaprep(p, b, t, blk.dsaprep_rows, prep_consts)
    y_a = _dsa(p, prep, tri_q, b, t, min(TOPK_MAX, t // 4))
    return _outproj(y_r, y_a, x2, w_out.astype(BF16), row2(final_g), blk.outproj_rows)


def kernel(x, norm_g, w_in, mu_shift, w0, w_up, a0, a_up, k_k, k_a, r_k, gn_g, gn_b,
           q_norm_g, kv_norm_g, w_uq, w_uk, w_uv, w_qidx, kidx_g, kidx_b, w_out, final_g):
    b, t, d = x.shape
    assert norm_g.shape[0] == 1, "single-layer problem"
    out = _layer(x.reshape(b * t, d), b, t, norm_g[0], w_in[0], mu_shift[0], w0[0], w_up[0],
                 a0[0], a_up[0], k_k[0], k_a[0], r_k[0], gn_g[0], gn_b[0], q_norm_g[0],
                 kv_norm_g[0], w_uq[0], w_uk[0], w_uv[0], w_qidx[0], kidx_g[0], kidx_b[0],
                 w_out[0], final_g)
    return out.reshape(b, t, d)
```
